```python
import jax, jax.numpy as jnp
from jax import lax
import numpy as np

D_MODEL = 1024
BATCH = 8
SEQ = 4096
DEPTH = 2
DEC_BATCH = 128
DEC_SEQ = 8
PAST_LEN = 16384
PAGE_SIZE = 128

N_EVEN = (DEPTH + 1) // 2
N_ODD = DEPTH // 2
ALPHA = (2 * DEPTH) ** 0.25
BETA = (8 * DEPTH) ** -0.25
LN_EPS = 1e-5
RMS_EPS = 1e-6
F32 = jnp.float32

A_WIDTH = D_MODEL // 2
POOL_WINDOWS = (2, 4, 8, 16)
A_GROUPS = len(POOL_WINDOWS)
A_GROUP_DIM = A_WIDTH // A_GROUPS
POOL_BUF = max(POOL_WINDOWS) - 1
B_WIDTH = D_MODEL // 2
B_HEADS = 4
B_HEAD_DIM = B_WIDTH // B_HEADS
CHUNK = 128
EVEN_IN = A_WIDTH + 2 * B_WIDTH
C_WIDTH = D_MODEL // 2
C_HEAD_DIM = 64
C_HEADS = C_WIDTH // C_HEAD_DIM
C_DECAY_LORA = 64
C_ICLR_LORA = 64
C_GATE_LORA = 128
C_COLS = 3 * C_WIDTH + C_DECAY_LORA + C_ICLR_LORA + C_GATE_LORA
RWKV_GN_EPS = 64e-5
D_HEADS = 8
D_NOPE = 64
D_ROPE = 32
D_VDIM = 64
Q_LORA = 256
KV_LORA = 128
D_WIDTH = D_HEADS * D_VDIM
D_COLS = Q_LORA + KV_LORA + D_ROPE
ODD_IN = C_COLS + D_COLS
MLA_SCALE = (D_NOPE + D_ROPE) ** -0.5
ROPE_BASE = 10000.0
Q_BLOCK = 128
N_EXPERTS = 16
N_EXPERT_GROUPS = 4
EXPERTS_PER_GROUP = N_EXPERTS // N_EXPERT_GROUPS
TOP_K = 2
D_EXPERT = 256

kernel_name = 'hybrid_pool_sgu_rwkv7_mla_moe_step'


def _layernorm(x, g, b):
    xf = x.astype(F32)
    mu = xf.mean(-1, keepdims=True)
    var = jnp.mean(jnp.square(xf - mu), -1, keepdims=True)
    return ((xf - mu) * lax.rsqrt(var + LN_EPS) * g + b).astype(x.dtype)


def _rmsnorm(x, g):
    xf = x.astype(F32)
    return (xf * lax.rsqrt(jnp.mean(xf * xf, -1, keepdims=True) + RMS_EPS) * g).astype(x.dtype)


def _group_norm(x, n_groups, g, b, eps):
    xf = x.astype(F32).reshape(x.shape[:-1] + (n_groups, x.shape[-1] // n_groups))
    mu = xf.mean(-1, keepdims=True)
    var = jnp.mean(jnp.square(xf - mu), -1, keepdims=True)
    xn = ((xf - mu) * lax.rsqrt(var + eps)).reshape(x.shape)
    return (xn * g + b).astype(x.dtype)


def _rope(x, positions):
    half = x.shape[-1] // 2
    inv = ROPE_BASE ** (-jnp.arange(half, dtype=F32) / half)
    ang = (positions[:, None] * inv).reshape((positions.shape[0],) + (1,) * (x.ndim - 3) + (half,))
    cos, sin = jnp.cos(ang), jnp.sin(ang)
    xf = x.astype(F32)
    x1, x2 = xf[..., :half], xf[..., half:]
    return jnp.concatenate([x1 * cos - x2 * sin, x1 * sin + x2 * cos], -1).astype(x.dtype)


def _multi_scale_pool(a, start_pos):
    L = a.shape[1]
    af = a.astype(F32)
    cs = jnp.pad(jnp.cumsum(af, axis=1), ((0, 0), (1, 0), (0, 0)))
    pos = start_pos + jnp.arange(L)
    outs = []
    for g, w in enumerate(POOL_WINDOWS):
        sl = slice(g * A_GROUP_DIM, (g + 1) * A_GROUP_DIM)
        hi = cs[:, 1:, sl]
        lo = jnp.pad(cs[:, :L + 1 - w, sl], ((0, 0), (w - 1, 0), (0, 0)))
        cnt = jnp.minimum(w, pos + 1).astype(F32)
        outs.append((hi - lo) / cnt[None, :, None])
    return jnp.concatenate(outs, -1) - af


def _chunk_spatial_mix(v, w_s, b_s):
    B, L, _ = v.shape
    cl = min(L, CHUNK)
    n = -(-L // cl)
    vh = jnp.pad(v, ((0, 0), (0, n * cl - L), (0, 0))).reshape(B, n, cl, B_HEADS, B_HEAD_DIM)
    w = w_s[:, :cl, :cl] * jnp.tril(jnp.ones((cl, cl), w_s.dtype))
    mix = jnp.einsum('hij,bnjhc->bnihc', w, vh) + b_s[:, :cl].T[None, None, :, :, None]
    return mix.reshape(B, n * cl, B_WIDTH)[:, :L]


def _even_mixer(x, pool_buf, start_pos, w_in, pool_w, pool_scale, sgu_g, sgu_bn, sgu_w, sgu_b, w_out):
    B, L, _ = x.shape
    p = x @ w_in
    a = p[..., :A_WIDTH]
    z = jax.nn.gelu(p[..., A_WIDTH:], approximate=False)
    u, v = z[..., :B_WIDTH], z[..., B_WIDTH:]
    a_ext = a if pool_buf is None else jnp.concatenate([pool_buf.astype(a.dtype), a], axis=1)
    ext_start = start_pos - (a_ext.shape[1] - L)
    pooled = _multi_scale_pool(a_ext, ext_start)[:, -L:].reshape(B, L, A_GROUPS, A_GROUP_DIM)
    a_out = (jnp.einsum('blgc,gcd->blgd', pooled, pool_w).reshape(B, L, A_WIDTH) * pool_scale).astype(x.dtype)
    vn = _group_norm(v, B_HEADS, sgu_g, sgu_bn, LN_EPS)
    b_out = u * _chunk_spatial_mix(vn, sgu_w, sgu_b)
    y = jnp.concatenate([a_out, b_out], -1) @ w_out
    return y, a_ext[:, -POOL_BUF:], vn


def _rwkv7(pc, shift_in, S_in, mu, w0, w2, a0, a2, g2, kkc, kac, rk, gn_g, gn_b):
    B, L, _ = pc.shape
    prev = jnp.concatenate([shift_in.astype(pc.dtype), pc[:, :-1]], axis=1)
    xs = (pc + (prev - pc) * mu).astype(F32)
    C, o1, o2 = C_WIDTH, 3 * C_WIDTH + C_DECAY_LORA, 3 * C_WIDTH + C_DECAY_LORA + C_ICLR_LORA
    r, k, v = xs[..., :C], xs[..., C:2 * C], xs[..., 2 * C:3 * C]
    wl, al, gl = xs[..., 3 * C:o1], xs[..., o1:o2], xs[..., o2:]
    log_w = -jax.nn.softplus(-(w0 + jnp.tanh(wl) @ w2)) - 0.5
    decay = jnp.exp(-jnp.exp(log_w))
    a = jax.nn.sigmoid(a0 + al @ a2)
    g = jax.nn.sigmoid(gl) @ g2
    heads = lambda t: t.reshape(B, L, C_HEADS, C_HEAD_DIM)
    kk = heads(k * kkc)
    kk = kk * lax.rsqrt(jnp.sum(kk * kk, -1, keepdims=True) + 1e-12)
    k = k * (1.0 + (a - 1.0) * kac)
    rh, wh, kh, vh, ah = heads(r), heads(decay), heads(k), heads(v), heads(a)

    def step(S, inp):
        r_t, w_t, k_t, v_t, kk_t, a_t = inp
        sa = jnp.einsum('bhvk,bhk->bhv', S, kk_t)
        S = S * w_t[:, :, None, :] - sa[..., None] * (kk_t * a_t)[:, :, None, :] + v_t[..., None] * k_t[:, :, None, :]
        return S, jnp.einsum('bhvk,bhk->bhv', S, r_t)

    seq = tuple(jnp.moveaxis(t, 1, 0) for t in (rh, wh, kh, vh, kk, ah))
    S_out, y = lax.scan(step, S_in.astype(F32), seq)
    y = jnp.moveaxis(y, 0, 1).reshape(B, L, C)
    yn = _group_norm(y, C_HEADS, gn_g, gn_b, RWKV_GN_EPS)
    bonus = jnp.sum(rh * kh * rk.reshape(C_HEADS, C_HEAD_DIM), -1, keepdims=True) * vh
    out = (yn + bonus.reshape(B, L, C)) * g
    return out.astype(pc.dtype), S_out, pc[:, -1:]


def _mla_project(pd, positions, gq, gkv, wuq, wuk):
    B, L, _ = pd.shape
    cq = _rmsnorm(pd[..., :Q_LORA], gq)
    ckv = _rmsnorm(pd[..., Q_LORA:Q_LORA + KV_LORA], gkv)
    kpe = _rope(pd[..., Q_LORA + KV_LORA:], positions)
    q = (cq @ wuq).reshape(B, L, D_HEADS, D_NOPE + D_ROPE)
    q_pe = _rope(q[..., D_NOPE:], positions)
    q_lat = jnp.einsum('blhn,hcn->blhc', q[..., :D_NOPE], wuk)
    return q_lat, q_pe, ckv, kpe


def _mla_scores(q_lat, q_pe, ckv, kpe):
    s = jnp.einsum('bqhc,bkc->bhqk', q_lat.astype(F32), ckv.astype(F32))
    s = s + jnp.einsum('bqhr,bkr->bhqk', q_pe.astype(F32), kpe.astype(F32))
    return s * MLA_SCALE


def _online_update(carry, s, vals):
    m, l, acc = carry
    m_new = jnp.maximum(m, s.max(-1))
    corr = jnp.exp(m - m_new)
    p = jnp.exp(s - m_new[..., None])
    l = l * corr + p.sum(-1)
    acc = acc * corr[..., None] + jnp.einsum('bhqk,bkc->bhqc', p, vals.astype(F32))
    return (m_new, l, acc)


def _mla_attend_prompt(q_lat, q_pe, ckv, kpe):
    L = q_lat.shape[1]
    outs = []
    for s0 in range(0, L, Q_BLOCK):
        e = min(s0 + Q_BLOCK, L)
        s = _mla_scores(q_lat[:, s0:e], q_pe[:, s0:e], ckv[:, :e], kpe[:, :e])
        causal = jnp.arange(s0, e)[:, None] >= jnp.arange(e)[None, :]
        prob = jax.nn.softmax(jnp.where(causal, s, -jnp.inf), axis=-1)
        outs.append(jnp.einsum('bhqk,bkc->bqhc', prob, ckv[:, :e].astype(F32)))
    return jnp.concatenate(outs, axis=1)


def _mla_attend_paged(q_lat, q_pe, ckv, kpe, pool_ckv, pool_kpe, page_table):
    B, L = q_lat.shape[:2]

    def page_step(carry, pages):
        ck = pool_ckv[pages]
        kp = pool_kpe[pages]
        return _online_update(carry, _mla_scores(q_lat, q_pe, ck, kp), ck), None

    init = (jnp.full((B, D_HEADS, L), -jnp.inf, F32), jnp.zeros((B, D_HEADS, L), F32),
            jnp.zeros((B, D_HEADS, L, KV_LORA), F32))
    carry, _ = lax.scan(page_step, init, page_table.T)
    causal = jnp.arange(L)[:, None] >= jnp.arange(L)[None, :]
    s = jnp.where(causal, _mla_scores(q_lat, q_pe, ckv, kpe), -jnp.inf)
    m, l, acc = _online_update(carry, s, ckv)
    return jnp.moveaxis(acc / l[..., None], 1, 2)


def _odd_mixer(x, shift_in, S_in, positions, paged, w_in, mu, w0, w2, a0, a2, g2, kkc, kac, rk,
               gn_g, gn_b, gq, gkv, wuq, wuk, wuv, w_out):
    B, L, _ = x.shape
    p = x @ w_in
    y_c, S_out, shift_out = _rwkv7(p[..., :C_COLS], shift_in, S_in, mu, w0, w2, a0, a2, g2,
                                   kkc, kac, rk, gn_g, gn_b)
    q_lat, q_pe, ckv, kpe = _mla_project(p[..., C_COLS:], positions, gq, gkv, wuq, wuk)
    if paged is None:
        o_lat = _mla_attend_prompt(q_lat, q_pe, ckv, kpe)
    else:
        o_lat = _mla_attend_paged(q_lat, q_pe, ckv, kpe, paged[0], paged[1], paged[2])
    y_d = jnp.einsum('blhc,hcn->blhn', o_lat, wuv).reshape(B, L, D_WIDTH).astype(x.dtype)
    y = jnp.concatenate([y_c, y_d], -1) @ w_out
    return y, S_out, shift_out, ckv, kpe


def _moe(x, w_router, router_bias, w_gate, w_up, w_down):
    shape = x.shape
    h = x.reshape(-1, D_MODEL)
    T = h.shape[0]
    scores = jax.nn.sigmoid(h.astype(F32) @ w_router.astype(F32))
    biased = scores + router_bias.astype(F32)
    grp_top = lax.top_k(biased.reshape(T, N_EXPERT_GROUPS, EXPERTS_PER_GROUP), TOP_K)[0]
    group = jnp.argmax(grp_top.sum(-1), axis=-1)
    expert_group = jnp.arange(N_EXPERTS) // EXPERTS_PER_GROUP
    masked = jnp.where(expert_group[None, :] == group[:, None], biased, -jnp.inf)
    _, idx = lax.top_k(masked, TOP_K)
    wts = jnp.take_along_axis(scores, idx, axis=-1)
    wts = wts / wts.sum(-1, keepdims=True)
    gates = jnp.sum(jax.nn.one_hot(idx, N_EXPERTS, dtype=F32) * wts[..., None], axis=1).astype(h.dtype)
    y = jnp.zeros_like(h)
    for e in range(N_EXPERTS):
        he = jax.nn.silu(h @ w_gate[e]) * (h @ w_up[e])
        y = y + gates[:, e:e + 1] * (he @ w_down[e])
    return y.reshape(shape)


def setup_inputs(seed: int = 0) -> dict:
    key = jax.random.key(seed)
    keys = iter(jax.random.split(key, 48))

    def nrm(shape, scale):
        return jax.random.normal(next(keys), shape, F32) * scale

    def near_one(shape):
        return 1.0 + nrm(shape, 0.05)

    n_pages = PAST_LEN // PAGE_SIZE
    n_pool = (DEC_BATCH * n_pages * 5) // 4
    x_prompt = nrm((BATCH, SEQ, D_MODEL), 1.0)
    x_sample = nrm((DEC_BATCH, DEC_SEQ, D_MODEL), 1.0)
    state_pool_buf = nrm((N_EVEN, DEC_BATCH, POOL_BUF, A_WIDTH), 1.0)
    state_rwkv = nrm((N_ODD, DEC_BATCH, C_HEADS, C_HEAD_DIM, C_HEAD_DIM), 0.2)
    state_shift = nrm((N_ODD, DEC_BATCH, 1, C_COLS), 1.0)
    cache_ckv = nrm((N_ODD, n_pool, PAGE_SIZE, KV_LORA), 1.0)
    cache_kpe = nrm((N_ODD, n_pool, PAGE_SIZE, D_ROPE), 1.0)
    perm = jax.random.permutation(next(keys), n_pool)
    page_table = perm[:DEC_BATCH * n_pages].reshape(DEC_BATCH, n_pages).astype(jnp.int32)
    return {
        'x_prompt': x_prompt,
        'x_sample': x_sample,
        'state_pool_buf': state_pool_buf,
        'state_rwkv': state_rwkv,
        'state_shift': state_shift,
        'cache_ckv': cache_ckv,
        'cache_kpe': cache_kpe,
        'page_table': page_table,
        'ln_g': near_one((DEPTH, 2, D_MODEL)),
        'ln_b': nrm((DEPTH, 2, D_MODEL), 0.02),
        'w_in_even': nrm((N_EVEN, D_MODEL, EVEN_IN), D_MODEL ** -0.5),
        'pool_w': nrm((N_EVEN, A_GROUPS, A_GROUP_DIM, A_GROUP_DIM), A_GROUP_DIM ** -0.5),
        'pool_scale': near_one((N_EVEN, A_WIDTH)),
        'sgu_ln_g': near_one((N_EVEN, B_WIDTH)),
        'sgu_ln_b': nrm((N_EVEN, B_WIDTH), 0.02),
        'sgu_w': nrm((N_EVEN, B_HEADS, CHUNK, CHUNK), CHUNK ** -0.5),
        'sgu_b': near_one((N_EVEN, B_HEADS, CHUNK)),
        'w_out_even': nrm((N_EVEN, A_WIDTH + B_WIDTH, D_MODEL), BETA * (A_WIDTH + B_WIDTH) ** -0.5),
        'w_in_odd': nrm((N_ODD, D_MODEL, ODD_IN), D_MODEL ** -0.5),
        'rwkv_mu': jax.random.uniform(next(keys), (N_ODD, C_COLS), F32, 0.0, 1.0),
        'rwkv_w0': jax.random.uniform(next(keys), (N_ODD, C_WIDTH), F32, -3.0, 1.0),
        'rwkv_w2': nrm((N_ODD, C_DECAY_LORA, C_WIDTH), 0.1),
        'rwkv_a0': nrm((N_ODD, C_WIDTH), 0.1),
        'rwkv_a2': nrm((N_ODD, C_ICLR_LORA, C_WIDTH), C_ICLR_LORA ** -0.5),
        'rwkv_g2': nrm((N_ODD, C_GATE_LORA, C_WIDTH), C_GATE_LORA ** -0.5),
        'rwkv_kk': 0.85 + nrm((N_ODD, C_WIDTH), 0.05),
        'rwkv_ka': near_one((N_ODD, C_WIDTH)),
        'rwkv_rk': nrm((N_ODD, C_WIDTH), 0.1),
        'rwkv_gn_g': near_one((N_ODD, C_WIDTH)),
        'rwkv_gn_b': nrm((N_ODD, C_WIDTH), 0.02),
        'mla_gq': near_one((N_ODD, Q_LORA)),
        'mla_gkv': near_one((N_ODD, KV_LORA)),
        'mla_wuq': nrm((N_ODD, Q_LORA, D_HEADS * (D_NOPE + D_ROPE)), Q_LORA ** -0.5),
        'mla_wuk': nrm((N_ODD, D_HEADS, KV_LORA, D_NOPE), KV_LORA ** -0.5),
        'mla_wuv': nrm((N_ODD, D_HEADS, KV_LORA, D_VDIM), KV_LORA ** -0.5),
        'w_out_odd': nrm((N_ODD, C_WIDTH + D_WIDTH, D_MODEL), BETA * (C_WIDTH + D_WIDTH) ** -0.5),
        'w_router': nrm((D_MODEL, N_EXPERTS), D_MODEL ** -0.5),
        'router_bias': nrm((N_EXPERTS,), 0.01),
        'w_gate': nrm((DEPTH, N_EXPERTS, D_MODEL, D_EXPERT), D_MODEL ** -0.5),
        'w_up': nrm((DEPTH, N_EXPERTS, D_MODEL, D_EXPERT), D_MODEL ** -0.5),
        'w_down': nrm((DEPTH, N_EXPERTS, D_EXPERT, D_MODEL), BETA * D_EXPERT ** -0.5),
    }


def reference(x_prompt, x_sample, state_pool_buf, state_rwkv, state_shift, cache_ckv, cache_kpe, page_table,
              ln_g, ln_b, w_in_even, pool_w, pool_scale, sgu_ln_g, sgu_ln_b, sgu_w, sgu_b, w_out_even,
              w_in_odd, rwkv_mu, rwkv_w0, rwkv_w2, rwkv_a0, rwkv_a2, rwkv_g2, rwkv_kk, rwkv_ka, rwkv_rk,
              rwkv_gn_g, rwkv_gn_b, mla_gq, mla_gkv, mla_wuq, mla_wuk, mla_wuv, w_out_odd,
              w_router, router_bias, w_gate, w_up, w_down):
    past_len = page_table.shape[1] * PAGE_SIZE
    bp, lp = x_prompt.shape[:2]
    ls = x_sample.shape[1]
    pos_p = jnp.arange(lp, dtype=F32)
    pos_s = past_len + jnp.arange(ls, dtype=F32)
    xp, xs = x_prompt, x_sample
    pool_p, pool_s, sgu_s = [], [], []
    rwkv_p, rwkv_s, shift_p, shift_s = [], [], [], []
    ckv_p, ckv_s, kpe_p, kpe_s = [], [], [], []
    for layer in range(DEPTH):
        i = layer // 2
        if layer % 2 == 0:
            ew = (w_in_even[i], pool_w[i], pool_scale[i], sgu_ln_g[i], sgu_ln_b[i], sgu_w[i], sgu_b[i], w_out_even[i])
            yp, buf_p, _ = _even_mixer(xp, None, 0, *ew)
            ys, buf_s, v_s = _even_mixer(xs, state_pool_buf[i], past_len, *ew)
            pool_p.append(buf_p)
            pool_s.append(buf_s.astype(state_pool_buf.dtype))
            sgu_s.append(v_s)
        else:
            ow = (w_in_odd[i], rwkv_mu[i], rwkv_w0[i], rwkv_w2[i], rwkv_a0[i], rwkv_a2[i], rwkv_g2[i],
                  rwkv_kk[i], rwkv_ka[i], rwkv_rk[i], rwkv_gn_g[i], rwkv_gn_b[i], mla_gq[i], mla_gkv[i],
                  mla_wuq[i], mla_wuk[i], mla_wuv[i], w_out_odd[i])
            shift0 = jnp.zeros((bp, 1, C_COLS), xp.dtype)
            S0 = jnp.zeros((bp, C_HEADS, C_HEAD_DIM, C_HEAD_DIM), F32)
            yp, S_p, sh_p, c_p, k_p = _odd_mixer(xp, shift0, S0, pos_p, None, *ow)
            ys, S_s, sh_s, c_s, k_s = _odd_mixer(xs, state_shift[i], state_rwkv[i], pos_s,
                                                 (cache_ckv[i], cache_kpe[i], page_table), *ow)
            rwkv_p.append(S_p.astype(xp.dtype))
            rwkv_s.append(S_s.astype(state_rwkv.dtype))
            shift_p.append(sh_p)
            shift_s.append(sh_s.astype(state_shift.dtype))
            ckv_p.append(c_p)
            ckv_s.append(c_s.astype(cache_ckv.dtype))
            kpe_p.append(k_p)
            kpe_s.append(k_s.astype(cache_kpe.dtype))
        xp = _layernorm(ALPHA * xp + yp, ln_g[layer, 0], ln_b[layer, 0])
        xs = _layernorm(ALPHA * xs + ys, ln_g[layer, 0], ln_b[layer, 0])
        xp = _layernorm(ALPHA * xp + _moe(xp, w_router, router_bias, w_gate[layer], w_up[layer], w_down[layer]),
                        ln_g[layer, 1], ln_b[layer, 1])
        xs = _layernorm(ALPHA * xs + _moe(xs, w_router, router_bias, w_gate[layer], w_up[layer], w_down[layer]),
                        ln_g[layer, 1], ln_b[layer, 1])
    pool_buf_prompt = jnp.stack(pool_p)
    pool_buf_sample = jnp.stack(pool_s)
    sgu_v_sample = jnp.stack(sgu_s)
    rwkv_state_prompt = jnp.stack(rwkv_p)
    rwkv_state_sample = jnp.stack(rwkv_s)
    shift_prompt = jnp.stack(shift_p)
    shift_sample = jnp.stack(shift_s)
    ckv_prompt = jnp.stack(ckv_p)
    ckv_sample = jnp.stack(ckv_s)
    kpe_prompt = jnp.stack(kpe_p)
    kpe_sample = jnp.stack(kpe_s)
    return (xp, xs, pool_buf_prompt, pool_buf_sample, sgu_v_sample, rwkv_state_prompt, rwkv_state_sample,
            shift_prompt, shift_sample, ckv_prompt, ckv_sample, kpe_prompt, kpe_sample)
```

```python
import functools
import math

import jax
import jax.numpy as jnp
import numpy as np
from jax import lax
from jax.experimental import pallas as pl
from jax.experimental.pallas import tpu as pltpu

F32 = jnp.float32
BF16 = jnp.bfloat16

D_MODEL = 1024
DEPTH = 2
ALPHA = (2 * DEPTH) ** 0.25
LN_EPS = 1e-5
RMS_EPS = 1e-6
A_WIDTH = 512
POOL_WINDOWS = (2, 4, 8, 16)
HALO = 16
POOL_BUF = 15
B_WIDTH = 512
B_HEADS = 4
SGU_CHUNK = 128
C_WIDTH = 512
C_HEAD_DIM = 64
C_HEADS = 8
C_COLS = 1792
RWKV_GN_EPS = 64e-5
C_DECAY_LORA = 64
RWKV_CHUNK = 64
D_HEADS = 8
D_NOPE = 64
D_ROPE = 32
Q_LORA = 256
KV_LORA = 128
MLA_SCALE = (D_NOPE + D_ROPE) ** -0.5
ROPE_BASE = 10000.0
PAGE_SIZE = 128
N_EXPERTS = 16
D_EXPERT = 256
LANES = 128
VMEM_LIMIT = 56 * 1024 * 1024

NEG_INF = float("-inf")


def _cparams(sem):
    return pltpu.CompilerParams(dimension_semantics=sem, vmem_limit_bytes=VMEM_LIMIT)


def _dot(a, b):
    return jnp.dot(a.astype(BF16), b.astype(BF16), preferred_element_type=F32)


def _dot_nt(a, b):
    return lax.dot_general(a.astype(BF16), b.astype(BF16), (((1,), (1,)), ((), ())), preferred_element_type=F32)


def _dot_tn(a, b):
    return lax.dot_general(a.astype(BF16), b.astype(BF16), (((0,), (0,)), ((), ())), preferred_element_type=F32)


def _split3(a):
    a1 = a.astype(BF16)
    r1 = a - a1.astype(F32)
    a2 = r1.astype(BF16)
    a3 = (r1 - a2.astype(F32)).astype(BF16)
    return a1, a2, a3


def _split2(a):
    hi = a.astype(BF16)
    return hi, (a - hi.astype(F32)).astype(BF16)


def _dot3(a, b_hi, b_lo):
    a_hi, a_lo = _split2(a)
    return (jnp.dot(a_hi, b_hi, preferred_element_type=F32)
            + (jnp.dot(a_hi, b_lo, preferred_element_type=F32) + jnp.dot(a_lo, b_hi, preferred_element_type=F32)))


def _layernorm(r, g, b):
    mu = jnp.mean(r, axis=-1, keepdims=True)
    d = r - mu
    var = jnp.mean(d * d, axis=-1, keepdims=True)
    return d * lax.rsqrt(var + LN_EPS) * g + b


def _rmsnorm(x, g):
    return x * lax.rsqrt(jnp.mean(x * x, axis=-1, keepdims=True) + RMS_EPS) * g


def _gelu(x):
    return 0.5 * x * (1.0 + lax.erf(x * np.float32(math.sqrt(0.5))))


def _even_tail(x, z, pooled, wrefs, tm):
    pw_ref, ps_ref, sg_ref, sb_ref, sw_ref, sbc_ref, woh_ref, wol_ref, lg_ref, lb_ref = wrefs
    u, v = z[:, :B_WIDTH], z[:, B_WIDTH:]
    a_out = jnp.concatenate(
        [_dot3(pooled[g], *_split2(pw_ref[g])) for g in range(len(POOL_WINDOWS))], axis=1) * ps_ref[...]
    row = lax.broadcasted_iota(jnp.int32, (SGU_CHUNK, SGU_CHUNK), 0)
    col = lax.broadcasted_iota(jnp.int32, (SGU_CHUNK, SGU_CHUNK), 1)
    vn_parts, b_parts = [], []
    for h in range(B_HEADS):
        sl = slice(h * LANES, (h + 1) * LANES)
        vh = v[:, sl]
        mu = jnp.mean(vh, axis=-1, keepdims=True)
        d = vh - mu
        var = jnp.mean(d * d, axis=-1, keepdims=True)
        vn = d * lax.rsqrt(var + LN_EPS) * sg_ref[:, sl] + sb_ref[:, sl]
        vn_parts.append(vn)
        wm = jnp.where(col <= row, sw_ref[h], 0.0)
        bias = sbc_ref[:, h:h + 1]
        v_hi, v_lo = _split2(vn)
        mix = jnp.concatenate(
            [_dot3(wm, v_hi[n * SGU_CHUNK:(n + 1) * SGU_CHUNK], v_lo[n * SGU_CHUNK:(n + 1) * SGU_CHUNK]) + bias
             for n in range(tm // SGU_CHUNK)], axis=0)
        b_parts.append(u[:, sl] * mix)
    vn_all = jnp.concatenate(vn_parts, axis=1)
    cat = jnp.concatenate([a_out] + b_parts, axis=1)
    y = _dot3(cat, woh_ref[...], wol_ref[...])
    return _layernorm(ALPHA * x + y, lg_ref[...], lb_ref[...]), vn_all


def _even_prompt_kernel(x_ref, wih_ref, wil_ref, *rest, tm):
    wrefs, (xo_ref, tail_ref, abuf) = rest[:-3], rest[-3:]
    t = pl.program_id(1)
    x = x_ref[...]
    p = _dot3(x, wih_ref[...], wil_ref[...])
    a = p[:, :A_WIDTH]
    z = _gelu(p[:, A_WIDTH:])

    @pl.when(t == 0)
    def _():
        abuf[0:HALO, :] = jnp.zeros((HALO, A_WIDTH), F32)

    abuf[HALO:HALO + tm, :] = a
    pos = t * tm + lax.broadcasted_iota(jnp.int32, (tm, LANES), 0)
    pooled = []
    for g, w in enumerate(POOL_WINDOWS):
        sl = slice(g * LANES, (g + 1) * LANES)
        acc = a[:, sl]
        for s in range(1, w):
            acc = acc + abuf[HALO - s:HALO - s + tm, sl]
        cnt = jnp.minimum(w, pos + 1).astype(F32)
        pooled.append(acc / cnt - a[:, sl])
    tail = abuf[tm:tm + HALO, :]
    abuf[0:HALO, :] = tail
    tail_ref[...] = tail
    xo, _ = _even_tail(x, z, pooled, wrefs, tm)
    xo_ref[...] = xo


def _even_sample_kernel(x_ref, buf_ref, wih_ref, wil_ref, *rest, tm, ls):
    wrefs, (xo_ref, anew_ref, vn_ref, ext) = rest[:-4], rest[-4:]
    nseq = tm // ls
    x = x_ref[...]
    p = _dot3(x, wih_ref[...], wil_ref[...])
    a = p[:, :A_WIDTH]
    z = _gelu(p[:, A_WIDTH:])
    ext[:, 0:HALO, :] = buf_ref[...]
    ext[:, HALO:HALO + ls, :] = a.reshape(nseq, ls, A_WIDTH)
    pooled = []
    for g, w in enumerate(POOL_WINDOWS):
        sl = slice(g * LANES, (g + 1) * LANES)
        acc = ext[:, HALO:HALO + ls, sl]
        for s in range(1, w):
            acc = acc + ext[:, HALO - s:HALO - s + ls, sl]
        pooled.append(acc.reshape(tm, LANES) * np.float32(1.0 / w) - a[:, sl])
    xo, vn = _even_tail(x, z, pooled, wrefs, tm)
    xo_ref[...] = xo
    anew_ref[...] = a
    vn_ref[...] = vn


def _const_spec(shape):
    nd = len(shape)
    return pl.BlockSpec(shape, lambda *_: (0,) * nd)


def _split_weight_kernel(w_ref, hi_ref, lo_ref):
    hi, lo = _split2(w_ref[...])
    hi_ref[...] = hi
    lo_ref[...] = lo


def _split_weight(w):
    rows, cols = w.shape
    tr = _tile_rows(rows, 256)
    spec = pl.BlockSpec((tr, cols), lambda i: (i, 0))
    return pl.pallas_call(
        _split_weight_kernel,
        grid=(rows // tr,),
        in_specs=[spec],
        out_specs=[spec, spec],
        out_shape=[jax.ShapeDtypeStruct(w.shape, BF16)] * 2,
        compiler_params=_cparams(("arbitrary",)),
        name="split_weight",
    )(w)


def _even_weights(w_in_split, pool_w, pool_scale, sgu_g, sgu_b, sgu_w, sgu_bcol, w_out_split, ln_g, ln_b):
    arrs = [*w_in_split, pool_w, pool_scale.reshape(1, -1), sgu_g.reshape(1, -1),
            sgu_b.reshape(1, -1), sgu_w, sgu_bcol, *w_out_split, ln_g.reshape(1, -1), ln_b.reshape(1, -1)]
    return arrs, [_const_spec(a.shape) for a in arrs]


def _even_prompt(x2d, nb, seq, tm, wts):
    nt = seq // tm
    arrs, specs = wts
    row_spec = lambda w: pl.BlockSpec((tm, w), lambda b, t: (b * nt + t, 0))
    return pl.pallas_call(
        functools.partial(_even_prompt_kernel, tm=tm),
        grid=(nb, nt),
        in_specs=[row_spec(D_MODEL)] + specs,
        out_specs=[row_spec(D_MODEL), pl.BlockSpec((HALO, A_WIDTH), lambda b, t: (b, 0))],
        out_shape=[jax.ShapeDtypeStruct((nb * seq, D_MODEL), F32), jax.ShapeDtypeStruct((nb * HALO, A_WIDTH), F32)],
        scratch_shapes=[pltpu.VMEM((tm + HALO, A_WIDTH), F32)],
        compiler_params=_cparams(("arbitrary", "arbitrary")),
        name="even_prompt",
    )(x2d, *arrs)


def _even_sample(x2d, bufpad, ls, tm, wts):
    rows = x2d.shape[0]
    nseq = tm // ls
    arrs, specs = wts
    row_spec = lambda w: pl.BlockSpec((tm, w), lambda i: (i, 0))
    return pl.pallas_call(
        functools.partial(_even_sample_kernel, tm=tm, ls=ls),
        grid=(rows // tm,),
        in_specs=[row_spec(D_MODEL), pl.BlockSpec((nseq, HALO, A_WIDTH), lambda i: (i, 0, 0))] + specs,
        out_specs=[row_spec(D_MODEL), row_spec(A_WIDTH), row_spec(B_WIDTH)],
        out_shape=[jax.ShapeDtypeStruct((rows, D_MODEL), F32), jax.ShapeDtypeStruct((rows, A_WIDTH), F32),
                   jax.ShapeDtypeStruct((rows, B_WIDTH), F32)],
        scratch_shapes=[pltpu.VMEM((nseq, HALO + ls, A_WIDTH), F32)],
        compiler_params=_cparams(("arbitrary",)),
        name="even_sample",
    )(x2d, bufpad, *arrs)


def _route(scores, bias):
    shape = scores.shape
    lane = lax.broadcasted_iota(jnp.int32, shape, 1)
    pos_in_group = lane % 4
    group = lane // 4
    biased = scores + bias

    def from_lane(x, d):
        return pltpu.roll(x, d % LANES, 1)

    offsets = (-3, -2, -1, 1, 2, 3)
    rank = jnp.zeros(shape, jnp.int32)
    for d in offsets:
        src = pos_in_group - d
        valid = (src >= 0) & (src <= 3)
        other = from_lane(biased, d)
        beats = (other > biased) | ((other == biased) & (d > 0))
        rank = rank + jnp.where(valid & beats, 1, 0)
    top2 = rank < 2
    kept = jnp.where(top2, biased, 0.0)
    gsum = kept
    for d in offsets:
        src = pos_in_group - d
        valid = (src >= 0) & (src <= 3)
        gsum = gsum + jnp.where(valid, from_lane(kept, d), 0.0)
    lost = jnp.zeros(shape, jnp.int32)
    for dg in offsets:
        src = group - dg
        valid = (src >= 0) & (src <= 3)
        other = from_lane(gsum, 4 * dg)
        beats = (other > gsum) | ((other == gsum) & (dg > 0))
        lost = lost + jnp.where(valid & beats, 1, 0)
    selected = top2 & (lost == 0) & (lane < N_EXPERTS)
    picked = jnp.where(selected, scores, 0.0)
    return picked / jnp.sum(picked, axis=-1, keepdims=True)


def _moe_kernel(x_ref, wr_ref, rb_ref, wg_ref, wu_ref, wd_ref, lg_ref, lb_ref, o_ref, acc, gates, xb):
    e = pl.program_id(1)

    @pl.when(e == 0)
    def _():
        x = x_ref[...]
        x1 = x.astype(BF16)
        x2 = (x - x1.astype(F32)).astype(BF16)
        wr = wr_ref[...]
        w1 = wr.astype(BF16)
        w2 = (wr - w1.astype(F32)).astype(BF16)
        logits = (jnp.dot(x1, w1, preferred_element_type=F32)
                  + (jnp.dot(x1, w2, preferred_element_type=F32) + jnp.dot(x2, w1, preferred_element_type=F32)))
        gates[...] = _route(jax.nn.sigmoid(logits), rb_ref[...])
        xb[...] = x1
        acc[...] = jnp.zeros(acc.shape, F32)

    xv = xb[...]
    hg = jnp.dot(xv, wg_ref[0], preferred_element_type=F32)
    hu = jnp.dot(xv, wu_ref[0], preferred_element_type=F32)
    he = hg * jax.nn.sigmoid(hg) * hu
    y = _dot(he, wd_ref[0])
    lane = lax.broadcasted_iota(jnp.int32, gates.shape, 1)
    ge = jnp.sum(jnp.where(lane == e, gates[...], 0.0), axis=-1, keepdims=True)
    acc[...] += ge * y

    @pl.when(e == N_EXPERTS - 1)
    def _():
        o_ref[...] = _layernorm(ALPHA * x_ref[...] + acc[...], lg_ref[...], lb_ref[...])


def _moe(x2d, tm, wr_pad, rb_pad, wg, wu, wd, ln_g, ln_b):
    rows = x2d.shape[0]
    row_spec = pl.BlockSpec((tm, D_MODEL), lambda i, e: (i, 0))
    return pl.pallas_call(
        _moe_kernel,
        grid=(rows // tm, N_EXPERTS),
        in_specs=[row_spec, _const_spec(wr_pad.shape), _const_spec(rb_pad.shape),
                  pl.BlockSpec((1, D_MODEL, D_EXPERT), lambda i, e: (e, 0, 0)),
                  pl.BlockSpec((1, D_MODEL, D_EXPERT), lambda i, e: (e, 0, 0)),
                  pl.BlockSpec((1, D_EXPERT, D_MODEL), lambda i, e: (e, 0, 0)),
                  _const_spec((1, D_MODEL)), _const_spec((1, D_MODEL))],
        out_specs=row_spec,
        out_shape=jax.ShapeDtypeStruct((rows, D_MODEL), F32),
        scratch_shapes=[pltpu.VMEM((tm, D_MODEL), F32), pltpu.VMEM((tm, LANES), F32), pltpu.VMEM((tm, D_MODEL), BF16)],
        compiler_params=_cparams(("arbitrary", "arbitrary")),
        name="moe",
    )(x2d, wr_pad, rb_pad, wg, wu, wd, ln_g.reshape(1, -1), ln_b.reshape(1, -1))


def _rope_lanes(x, cos_t, sin_t):
    lane = lax.broadcasted_iota(jnp.int32, x.shape, 1)
    partner = jnp.where((lane % D_ROPE) < D_ROPE // 2,
                        pltpu.roll(x, LANES - D_ROPE // 2, 1), pltpu.roll(x, D_ROPE // 2, 1))
    return x * cos_t + partner * sin_t


def _odd_in_kernel(x_ref, cos_ref, sin_ref, wi_ref, gq_ref, gkv_ref, wuq_ref, wuk_ref,
                   pc_ref, qlat_ref, qpe_ref, ckv_ref, kpe_ref, kcat_ref):
    p = _dot(x_ref[...], wi_ref[...])
    pc_ref[...] = p[:, :C_COLS]
    o = C_COLS
    cq = _rmsnorm(p[:, o:o + Q_LORA], gq_ref[...])
    ckv = _rmsnorm(p[:, o + Q_LORA:o + Q_LORA + KV_LORA], gkv_ref[...])
    cos_t, sin_t = cos_ref[...], sin_ref[...]
    kslot = _rope_lanes(p[:, o + Q_LORA + KV_LORA:o + Q_LORA + KV_LORA + LANES], cos_t, sin_t)
    q = _dot(cq, wuq_ref[...])
    nope = D_HEADS * D_NOPE
    qlat_ref[...] = _dot(q[:, :nope], wuk_ref[...]).astype(qlat_ref.dtype)
    qpe_ref[...] = jnp.concatenate(
        [_rope_lanes(q[:, nope + i * LANES:nope + (i + 1) * LANES], cos_t, sin_t) for i in range(2)],
        axis=1).astype(qpe_ref.dtype)
    ckv_ref[...] = ckv
    kpe_ref[...] = kslot[:, :D_ROPE]
    kcat_ref[...] = jnp.concatenate([ckv, kslot], axis=1).astype(kcat_ref.dtype)


def _odd_in(x2d, cos_t, sin_t, tm, pos_blocks, arrs, qdt):
    rows = x2d.shape[0]
    row_spec = lambda w: pl.BlockSpec((tm, w), lambda i: (i, 0))
    pos_spec = pl.BlockSpec((tm, LANES), lambda i: (i % pos_blocks, 0))
    out_w = [(C_COLS, F32), (D_HEADS * KV_LORA, qdt), (2 * LANES, qdt), (KV_LORA, F32), (D_ROPE, F32),
             (2 * LANES, qdt)]
    return pl.pallas_call(
        _odd_in_kernel,
        grid=(rows // tm,),
        in_specs=[row_spec(D_MODEL), pos_spec, pos_spec] + [_const_spec(a.shape) for a in arrs],
        out_specs=[row_spec(w) for w, _ in out_w],
        out_shape=[jax.ShapeDtypeStruct((rows, w), dt) for w, dt in out_w],
        compiler_params=_cparams(("arbitrary",)),
        name="odd_in",
    )(x2d, cos_t, sin_t, *arrs)


def _rwkv_kernel(pc_ref, shift_ref, p0_ref, mu_ref, w0_ref, wa2_ref, a0_ref, g2_ref, kkc_ref, kac_ref, rk_ref,
                 gng_ref, gnb_ref, y_ref, pout_ref, state, xbuf, *, chunk, n_valid):
    C = chunk
    C2 = 2 * C
    n = pl.program_id(1)

    @pl.when(n == 0)
    def _():
        state[...] = p0_ref[0]
        xbuf[7:8, :] = shift_ref[0]

    pc = pc_ref[...].reshape(n_valid, C_COLS)
    if n_valid < C:
        pc = jnp.concatenate([pc, jnp.zeros((C - n_valid, C_COLS), F32)], axis=0)
    xbuf[8:8 + C, :] = pc
    prev = xbuf[7:7 + C, :]
    xs = pc + (prev - pc) * mu_ref[...]
    xbuf[7:8, :] = pc[n_valid - 1:n_valid, :]

    W = C_WIDTH
    r, k, v = xs[:, :W], xs[:, W:2 * W], xs[:, 2 * W:3 * W]
    wa = xs[:, 3 * W:3 * W + LANES]
    gl = xs[:, 3 * W + LANES:]
    lane = lax.broadcasted_iota(jnp.int32, (C, LANES), 1)
    first_half = lane < C_HEAD_DIM
    lora = _dot(jnp.where(first_half, jnp.tanh(wa), wa), wa2_ref[...])
    zw = -(w0_ref[...] + lora[:, :W])
    softplus = jnp.maximum(zw, 0.0) + jnp.log1p(jnp.exp(-jnp.abs(zw)))
    ld = -jnp.exp(-softplus - 0.5)
    a = jax.nn.sigmoid(a0_ref[...] + lora[:, W:])
    g = _dot(jax.nn.sigmoid(gl), g2_ref[...])

    ri = lax.broadcasted_iota(jnp.int32, (LANES, LANES), 0)
    ci = lax.broadcasted_iota(jnp.int32, (LANES, LANES), 1)
    same_head = (ri // C_HEAD_DIM) == (ci // C_HEAD_DIM)
    head_ones = jnp.where(same_head, 1.0, 0.0).astype(BF16)
    eye = ri == ci

    def head_sum(xv):
        return jnp.concatenate(
            [_dot(xv[:, q * LANES:(q + 1) * LANES], head_ones) for q in range(W // LANES)], axis=1)

    kk = k * kkc_ref[...]
    kk = kk * lax.rsqrt(head_sum(kk * kk) + 1e-12)
    kp = k * (1.0 + (a - 1.0) * kac_ref[...])
    bonus = head_sum(r * kp * rk_ref[...]) * v
    if n_valid < C:
        valid = lax.broadcasted_iota(jnp.int32, (C, W), 0) < n_valid
        ld = jnp.where(valid, ld, 0.0)
        kk = jnp.where(valid, kk, 0.0)
        kp = jnp.where(valid, kp, 0.0)
        v = jnp.where(valid, v, 0.0)
    b = kk * a

    tr = lax.broadcasted_iota(jnp.int32, (C, C), 0)
    tc = lax.broadcasted_iota(jnp.int32, (C, C), 1)
    tri = jnp.where(tc <= tr, 1.0, 0.0).astype(BF16)
    l1, l2, l3 = _split3(ld)
    lam = (jnp.dot(tri, l1, preferred_element_type=F32) + jnp.dot(tri, l2, preferred_element_type=F32)
           + jnp.dot(tri, l3, preferred_element_type=F32))
    lam_end = lam[C - 1:C, :]
    e_neg = jnp.exp(-lam)
    e_end = jnp.exp(lam_end - lam)
    rt = r * jnp.exp(lam)
    kt = kp * e_neg
    kkt = kk * jnp.exp(lam - ld)
    bt = b * e_neg
    kg = kp * e_end
    bg = b * e_end
    g_end = jnp.exp(lam_end)

    r2 = lax.broadcasted_iota(jnp.int32, (C2, C2), 0)
    c2 = lax.broadcasted_iota(jnp.int32, (C2, C2), 1)
    same_blk = (r2 // C) == (c2 // C)
    strict = same_blk & ((c2 % C) < (r2 % C))
    incl = same_blk & ((c2 % C) <= (r2 % C))
    eye2 = jnp.where(r2 == c2, 1.0, 0.0)

    def stack(xv, q):
        xq = xv[:, q * LANES:(q + 1) * LANES]
        return jnp.concatenate([jnp.where(first_half, xq, 0.0), jnp.where(first_half, 0.0, xq)], axis=0)

    ys = []
    for q in range(W // LANES):
        gz, rz, bz, kz, vz = stack(kkt, q), stack(rt, q), stack(bt, q), stack(kt, q), stack(v, q)
        bgz, kgz = stack(bg, q), stack(kg, q)
        sc = _dot_nt(jnp.concatenate([gz, rz], axis=0), jnp.concatenate([bz, kz], axis=0))
        aab = jnp.where(strict, sc[:C2, :C2], 0.0)
        aak = jnp.where(strict, sc[:C2, C2:], 0.0)
        rrb = jnp.where(incl, sc[C2:, :C2], 0.0)
        rrk = jnp.where(incl, sc[C2:, C2:], 0.0)
        tinv = eye2 - aab
        xp = _dot(aab, aab)
        for _ in range(int(math.log2(C)) - 1):
            sq = _dot(xp, jnp.concatenate([tinv, xp], axis=1))
            tinv = tinv + sq[:, :C2]
            xp = sq[:, C2:]
        akv = _dot(aak, vz)
        wu = _dot(tinv, jnp.concatenate([gz, akv], axis=1))
        bwu = _dot_tn(bgz, wu)
        kv = _dot_tn(kgz, vz)
        g_end_q = g_end[:, q * LANES:(q + 1) * LANES]
        m_mat = jnp.where(eye, g_end_q, 0.0) - bwu[:, :LANES]
        n_mat = kv - bwu[:, LANES:]
        rwu = _dot(rrb, wu)
        rv = _dot(rrk, vz)
        qz = rz - rwu[:, :LANES]
        y0z = rv - rwu[:, LANES:]
        qm = qz[:C] + qz[C:]
        y0 = y0z[:C] + y0z[C:]
        p_old = state[q]
        yp = _dot(jnp.concatenate([qm, m_mat], axis=0), p_old)
        ys.append(yp[:C] + y0)
        state[q] = yp[C:] + n_mat
    y = jnp.concatenate(ys, axis=1)

    inv_n = np.float32(1.0 / C_HEAD_DIM)
    mean = head_sum(y) * inv_n
    d = y - mean
    var = head_sum(d * d) * inv_n
    yn = d * lax.rsqrt(var + RWKV_GN_EPS) * gng_ref[...] + gnb_ref[...]
    out = (yn + bonus) * g
    y_ref[...] = out[:n_valid].reshape(y_ref.shape)
    pout_ref[0] = state[...]


def _rwkv(pc, shift_in, p0, nb, n_chunks, n_valid, arrs):
    C = RWKV_CHUNK
    npair = C_WIDTH // LANES
    if n_valid == C:
        pc_in, pc_spec = pc, pl.BlockSpec((C, C_COLS), lambda b, n: (b * n_chunks + n, 0))
        y_shape, y_spec = (pc.shape[0], C_WIDTH), pl.BlockSpec((C, C_WIDTH), lambda b, n: (b * n_chunks + n, 0))
    else:
        pc_in = pc.reshape(nb * n_chunks, n_valid, C_COLS)
        pc_spec = pl.BlockSpec((1, n_valid, C_COLS), lambda b, n: (b * n_chunks + n, 0, 0))
        y_shape = (nb * n_chunks, n_valid, C_WIDTH)
        y_spec = pl.BlockSpec((1, n_valid, C_WIDTH), lambda b, n: (b * n_chunks + n, 0, 0))
    y, pout = pl.pallas_call(
        functools.partial(_rwkv_kernel, chunk=C, n_valid=n_valid),
        grid=(nb, n_chunks),
        in_specs=[pc_spec, pl.BlockSpec((1, 1, C_COLS), lambda b, n: (b, 0, 0)),
                  pl.BlockSpec((1, npair, LANES, LANES), lambda b, n: (b, 0, 0, 0))]
                 + [_const_spec(a.shape) for a in arrs],
        out_specs=[y_spec, pl.BlockSpec((1, npair, LANES, LANES), lambda b, n: (b, 0, 0, 0))],
        out_shape=[jax.ShapeDtypeStruct(y_shape, F32), jax.ShapeDtypeStruct((nb, npair, LANES, LANES), F32)],
        scratch_shapes=[pltpu.VMEM((npair, LANES, LANES), F32), pltpu.VMEM((C + 8, C_COLS), F32)],
        compiler_params=_cparams(("arbitrary", "arbitrary")),
        name="rwkv7",
    )(pc_in, shift_in, p0, *arrs)
    return y.reshape(-1, C_WIDTH), pout


def _state_to_pairs(s):
    nb = s.shape[0]
    p = jnp.swapaxes(s, -1, -2).reshape(nb, C_HEADS // 2, 2, C_HEAD_DIM, C_HEAD_DIM)
    z = jnp.zeros_like(p[:, :, 0])
    top = jnp.concatenate([p[:, :, 0], z], axis=-1)
    bot = jnp.concatenate([z, p[:, :, 1]], axis=-1)
    return jnp.concatenate([top, bot], axis=-2)


def _pairs_to_state(p):
    h0 = p[:, :, :C_HEAD_DIM, :C_HEAD_DIM]
    h1 = p[:, :, C_HEAD_DIM:, C_HEAD_DIM:]
    s = jnp.stack([h0, h1], axis=2).reshape(p.shape[0], C_HEADS, C_HEAD_DIM, C_HEAD_DIM)
    return jnp.swapaxes(s, -1, -2)


def _query_rows(qlat, qpe, slot_of_head):
    lane = lax.broadcasted_iota(jnp.int32, (qlat.shape[0], LANES), 1)
    rows = []
    for h in range(D_HEADS):
        half = qpe[:, (h // 4) * LANES:(h // 4 + 1) * LANES]
        src, dst = h % 4, slot_of_head(h)
        if dst != src:
            half = pltpu.roll(half, ((dst - src) * D_ROPE) % LANES, 1)
        slot = jnp.where((lane // D_ROPE) == dst, half, jnp.zeros_like(half))
        rows.append(jnp.concatenate([qlat[:, h * KV_LORA:(h + 1) * KV_LORA], slot], axis=1))
    return jnp.concatenate(rows, axis=0)


def _attn_prompt_kernel(qlat_ref, qpe_ref, k_ref, o_ref, qs, m_s, l_s, acc, *, tq, tk):
    qi, kj = pl.program_id(1), pl.program_id(2)
    nk = pl.num_programs(2)

    @pl.when(kj == 0)
    def _():
        qs[...] = _query_rows(qlat_ref[...], qpe_ref[...], lambda h: h % 4)
        m_s[...] = jnp.full(m_s.shape, NEG_INF, F32)
        l_s[...] = jnp.zeros(l_s.shape, F32)
        acc[...] = jnp.zeros(acc.shape, F32)

    @pl.when(kj * tk < (qi + 1) * tq)
    def _():
        kc = k_ref[...]
        s = lax.dot_general(qs[...], kc, (((1,), (1,)), ((), ())), preferred_element_type=F32) * MLA_SCALE
        rows = D_HEADS * tq
        qpos = qi * tq + (lax.broadcasted_iota(jnp.int32, (rows, tk), 0) % tq)
        kpos = kj * tk + lax.broadcasted_iota(jnp.int32, (rows, tk), 1)
        s = jnp.where(qpos >= kpos, s, NEG_INF)
        m_old = m_s[...]
        m_new = jnp.maximum(m_old, jnp.max(s, axis=-1, keepdims=True))
        corr = jnp.exp(m_old - m_new)
        p = jnp.exp(s - m_new)
        l_s[...] = l_s[...] * corr + jnp.sum(p, axis=-1, keepdims=True)
        acc[...] = acc[...] * corr + jnp.dot(p.astype(BF16), kc[:, :KV_LORA], preferred_element_type=F32)
        m_s[...] = m_new

    @pl.when(kj == nk - 1)
    def _():
        o = acc[...] / l_s[...]
        o_ref[...] = jnp.concatenate([o[h * tq:(h + 1) * tq] for h in range(D_HEADS)], axis=1).astype(BF16)


def _attn_prompt(qlat, qpe, kcat, nb, seq, tq, tk):
    nq, nk = seq // tq, seq // tk

    def k_index(b, qi, kj):
        last = ((qi + 1) * tq - 1) // tk
        return (b * nk + jnp.minimum(kj, last), 0)

    rows = D_HEADS * tq
    return pl.pallas_call(
        functools.partial(_attn_prompt_kernel, tq=tq, tk=tk),
        grid=(nb, nq, nk),
        in_specs=[pl.BlockSpec((tq, D_HEADS * KV_LORA), lambda b, qi, kj: (b * nq + qi, 0)),
                  pl.BlockSpec((tq, 2 * LANES), lambda b, qi, kj: (b * nq + qi, 0)),
                  pl.BlockSpec((tk, 2 * LANES), k_index)],
        out_specs=pl.BlockSpec((tq, D_HEADS * KV_LORA), lambda b, qi, kj: (b * nq + qi, 0)),
        out_shape=jax.ShapeDtypeStruct((nb * seq, D_HEADS * KV_LORA), BF16),
        scratch_shapes=[pltpu.VMEM((rows, 2 * LANES), BF16), pltpu.VMEM((rows, 1), F32), pltpu.VMEM((rows, 1), F32),
                        pltpu.VMEM((rows, KV_LORA), F32)],
        compiler_params=_cparams(("arbitrary", "arbitrary", "arbitrary")),
        name="attn_prompt",
    )(qlat, qpe, kcat)


def _attn_paged_kernel(pt_ref, qlat_ref, qpe_ref, knew_ref, ckv_hbm, kpe_hbm, o_ref,
                       ckv_buf, kpe_buf, sems, ql_s, qslot_s, m_s, l_s, acc, *, ls, gp, sub):
    b, j = pl.program_id(0), pl.program_id(1)
    nbatch, ngroups = pl.num_programs(0), pl.num_programs(1)
    step = b * ngroups + j
    slot = step % 2
    per_page = PAGE_SIZE // sub

    def page_copies(bb, jj, sl):
        out = []
        for g in range(gp):
            page = pt_ref[bb, jj * gp + g]
            out.append(pltpu.make_async_copy(ckv_hbm.at[page], ckv_buf.at[sl, pl.ds(g * PAGE_SIZE, PAGE_SIZE)],
                                             sems.at[sl]))
            out.append(pltpu.make_async_copy(kpe_hbm.at[page], kpe_buf.at[sl, pl.ds(g * per_page, per_page)],
                                             sems.at[sl]))
        return out

    @pl.when(step == 0)
    def _():
        for c in page_copies(b, j, slot):
            c.start()

    @pl.when(step + 1 < nbatch * ngroups)
    def _():
        nxt = step + 1
        for c in page_copies(nxt // ngroups, nxt % ngroups, 1 - slot):
            c.start()

    rows = D_HEADS * ls

    @pl.when(j == 0)
    def _():
        qlat = qlat_ref[0]
        qpe = qpe_ref[0]
        ql_s[...] = jnp.concatenate(
            [qlat[:, h * KV_LORA:(h + 1) * KV_LORA] for h in range(D_HEADS)], axis=0).astype(BF16)
        for m in range(sub):
            qslot_s[m] = _query_rows(qlat, qpe, lambda h: m)[:, KV_LORA:].astype(BF16)
        m_s[...] = jnp.full(m_s.shape, NEG_INF, F32)
        l_s[...] = jnp.zeros(l_s.shape, F32)
        acc[...] = jnp.zeros(acc.shape, F32)

    for c in page_copies(b, j, slot):
        c.wait()

    def online(s, vals):
        m_old = m_s[...]
        m_new = jnp.maximum(m_old, jnp.max(s, axis=-1, keepdims=True))
        corr = jnp.exp(m_old - m_new)
        p = jnp.exp(s - m_new)
        l_s[...] = l_s[...] * corr + jnp.sum(p, axis=-1, keepdims=True)
        acc[...] = acc[...] * corr + jnp.dot(p.astype(BF16), vals, preferred_element_type=F32)
        m_s[...] = m_new

    ql = ql_s[...]
    kpe_rows = kpe_buf[slot].astype(BF16)
    n_rows = gp * per_page
    for m in range(sub):
        ck = ckv_buf[slot, pl.ds(m, n_rows, stride=sub), :].astype(BF16)
        s = (lax.dot_general(ql, ck, (((1,), (1,)), ((), ())), preferred_element_type=F32)
             + lax.dot_general(qslot_s[m], kpe_rows, (((1,), (1,)), ((), ())), preferred_element_type=F32))
        online(s * MLA_SCALE, ck)

    @pl.when(j == ngroups - 1)
    def _():
        kn = knew_ref[0]
        pad = jnp.concatenate([kn, jnp.zeros((LANES - ls, 2 * LANES), F32)], axis=0).astype(BF16)
        s = (lax.dot_general(ql, pad[:, :KV_LORA], (((1,), (1,)), ((), ())), preferred_element_type=F32)
             + lax.dot_general(qslot_s[0], pad[:, KV_LORA:], (((1,), (1,)), ((), ())), preferred_element_type=F32))
        tq_pos = lax.broadcasted_iota(jnp.int32, (rows, LANES), 0) % ls
        tk_pos = lax.broadcasted_iota(jnp.int32, (rows, LANES), 1)
        s = jnp.where(tq_pos >= tk_pos, s * MLA_SCALE, NEG_INF)
        online(s, pad[:, :KV_LORA])
        o = acc[...] / l_s[...]
        o_ref[0] = jnp.concatenate([o[h * ls:(h + 1) * ls] for h in range(D_HEADS)], axis=1)


def _attn_paged(page_table, qlat3, qpe3, knew3, cache_ckv, cache_kpe_rows, gp):
    nbatch, ls = qlat3.shape[0], qlat3.shape[1]
    n_pages = page_table.shape[1]
    sub = cache_kpe_rows.shape[2] // D_ROPE
    per_page = PAGE_SIZE // sub
    rows = D_HEADS * ls
    grid_spec = pltpu.PrefetchScalarGridSpec(
        num_scalar_prefetch=1,
        grid=(nbatch, n_pages // gp),
        in_specs=[pl.BlockSpec((1, ls, D_HEADS * KV_LORA), lambda b, j, pt: (b, 0, 0)),
                  pl.BlockSpec((1, ls, 2 * LANES), lambda b, j, pt: (b, 0, 0)),
                  pl.BlockSpec((1, ls, 2 * LANES), lambda b, j, pt: (b, 0, 0)),
                  pl.BlockSpec(memory_space=pl.ANY), pl.BlockSpec(memory_space=pl.ANY)],
        out_specs=pl.BlockSpec((1, ls, D_HEADS * KV_LORA), lambda b, j, pt: (b, 0, 0)),
        scratch_shapes=[pltpu.VMEM((2, gp * PAGE_SIZE, KV_LORA), F32), pltpu.VMEM((2, gp * per_page, LANES), F32),
                        pltpu.SemaphoreType.DMA((2,)),
                        pltpu.VMEM((rows, KV_LORA), BF16), pltpu.VMEM((sub, rows, LANES), BF16),
                        pltpu.VMEM((rows, 1), F32), pltpu.VMEM((rows, 1), F32), pltpu.VMEM((rows, KV_LORA), F32)])
    return pl.pallas_call(
        functools.partial(_attn_paged_kernel, ls=ls, gp=gp, sub=sub),
        grid_spec=grid_spec,
        out_shape=jax.ShapeDtypeStruct((nbatch, ls, D_HEADS * KV_LORA), F32),
        compiler_params=_cparams(("arbitrary", "arbitrary")),
        name="attn_paged",
    )(page_table, qlat3, qpe3, knew3, cache_ckv, cache_kpe_rows)


def _odd_out_kernel(x_ref, yc_ref, ol_ref, wuv_ref, wo_ref, lg_ref, lb_ref, o_ref):
    yd = _dot(ol_ref[...], wuv_ref[...])
    y = _dot(jnp.concatenate([yc_ref[...], yd], axis=1), wo_ref[...])
    o_ref[...] = _layernorm(ALPHA * x_ref[...] + y, lg_ref[...], lb_ref[...])


def _odd_out(x2d, yc, olat, tm, arrs):
    rows = x2d.shape[0]
    row_spec = lambda w: pl.BlockSpec((tm, w), lambda i: (i, 0))
    return pl.pallas_call(
        _odd_out_kernel,
        grid=(rows // tm,),
        in_specs=[row_spec(D_MODEL), row_spec(C_WIDTH), row_spec(D_HEADS * KV_LORA)]
                 + [_const_spec(a.shape) for a in arrs],
        out_specs=row_spec(D_MODEL),
        out_shape=jax.ShapeDtypeStruct((rows, D_MODEL), F32),
        compiler_params=_cparams(("arbitrary",)),
        name="odd_out",
    )(x2d, yc, olat, *arrs)


def _block_diag(blocks):
    n, r, c = blocks.shape
    eye = jnp.eye(n, dtype=blocks.dtype)
    return (eye[:, None, :, None] * blocks[:, :, None, :]).reshape(n * r, n * c)


def _rope_tables(positions):
    half = D_ROPE // 2
    inv = ROPE_BASE ** (-jnp.arange(half, dtype=F32) / half)
    ang = positions[:, None] * inv
    cos, sin = jnp.cos(ang), jnp.sin(ang)
    reps = LANES // D_ROPE
    return jnp.tile(jnp.concatenate([cos, cos], axis=1), (1, reps)), jnp.tile(jnp.concatenate([-sin, sin], axis=1), (1, reps))


def _tile_rows(n, cap):
    t = min(n, cap)
    assert n % t == 0, (n, cap)
    return t


def kernel(x_prompt, x_sample, state_pool_buf, state_rwkv, state_shift, cache_ckv, cache_kpe, page_table, ln_g, ln_b, w_in_even, pool_w, pool_scale, sgu_ln_g, sgu_ln_b, sgu_w, sgu_b, w_out_even, w_in_odd, rwkv_mu, rwkv_w0, rwkv_w2, rwkv_a0, rwkv_a2, rwkv_g2, rwkv_kk, rwkv_ka, rwkv_rk, rwkv_gn_g, rwkv_gn_b, mla_gq, mla_gkv, mla_wuq, mla_wuk, mla_wuv, w_out_odd, w_router, router_bias, w_gate, w_up, w_down):
    bp, lp, _ = x_prompt.shape
    bs, ls, _ = x_sample.shape
    n_pages = page_table.shape[1]
    past_len = n_pages * PAGE_SIZE
    assert ls == 8 and lp % SGU_CHUNK == 0 and lp % RWKV_CHUNK == 0
    rows_p, rows_s = bp * lp, bs * ls
    tm_p = _tile_rows(lp, 512)
    tm_s = _tile_rows(rows_s, 512)
    tm_moe_p = _tile_rows(rows_p, 1024)
    tm_moe_s = _tile_rows(rows_s, 1024)

    xp = x_prompt.reshape(rows_p, D_MODEL)
    xs = x_sample.reshape(rows_s, D_MODEL)

    wr_pad = jnp.pad(w_router, ((0, 0), (0, LANES - N_EXPERTS)))
    rb_pad = jnp.pad(router_bias, (0, LANES - N_EXPERTS)).reshape(1, LANES)
    wg, wu, wd = w_gate.astype(BF16), w_up.astype(BF16), w_down.astype(BF16)

    outs = {}
    for layer in range(DEPTH):
        i = layer // 2
        g1, b1 = ln_g[layer, 0], ln_b[layer, 0]
        if layer % 2 == 0:
            wi_split, wo_split = _split_weight(w_in_even[i]), _split_weight(w_out_even[i])
            wts_p = _even_weights(wi_split, pool_w[i], pool_scale[i], sgu_ln_g[i], sgu_ln_b[i], sgu_w[i],
                                  sgu_b[i].T, wo_split, g1, b1)
            reps = SGU_CHUNK // ls
            w_small = jnp.stack([_block_diag(jnp.broadcast_to(sgu_w[i][h, :ls, :ls], (reps, ls, ls)))
                                 for h in range(B_HEADS)])
            b_small = jnp.tile(sgu_b[i][:, :ls], (1, reps)).T
            wts_s = _even_weights(wi_split, pool_w[i], pool_scale[i], sgu_ln_g[i], sgu_ln_b[i], w_small,
                                  b_small, wo_split, g1, b1)
            xp, tail = _even_prompt(xp, bp, lp, _tile_rows(lp, 256), wts_p)
            bufpad = jnp.pad(state_pool_buf[i], ((0, 0), (HALO - POOL_BUF, 0), (0, 0)))
            xs, a_new, vn_s = _even_sample(xs, bufpad, ls, tm_s, wts_s)
            outs.setdefault("pool_p", []).append(tail.reshape(bp, HALO, A_WIDTH)[:, HALO - POOL_BUF:])
            ext = jnp.concatenate([state_pool_buf[i], a_new.reshape(bs, ls, A_WIDTH)], axis=1)
            outs.setdefault("pool_s", []).append(ext[:, -POOL_BUF:])
            outs.setdefault("sgu_s", []).append(vn_s.reshape(bs, ls, B_WIDTH))
        else:
            nope = D_HEADS * D_NOPE
            wq_cols = mla_wuq[i].reshape(Q_LORA, D_HEADS, D_NOPE + D_ROPE)
            wuq_perm = jnp.concatenate([wq_cols[:, :, :D_NOPE].reshape(Q_LORA, nope),
                                        wq_cols[:, :, D_NOPE:].reshape(Q_LORA, D_HEADS * D_ROPE)], axis=1)
            wuk_bd = _block_diag(jnp.swapaxes(mla_wuk[i], 1, 2))
            wuv_bd = _block_diag(mla_wuv[i])
            w_in = w_in_odd[i]
            w_in_pad = jnp.concatenate([w_in[:, :C_COLS + Q_LORA + KV_LORA],
                                        jnp.tile(w_in[:, C_COLS + Q_LORA + KV_LORA:], (1, LANES // D_ROPE))], axis=1)
            in_arrs = [w_in_pad.astype(BF16), mla_gq[i].reshape(1, -1), mla_gkv[i].reshape(1, -1),
                       wuq_perm.astype(BF16), wuk_bd.astype(BF16)]
            cos_p, sin_p = _rope_tables(jnp.arange(lp, dtype=F32))
            cos_s, sin_s = _rope_tables(past_len + jnp.arange(ls, dtype=F32))
            cos_s, sin_s = jnp.tile(cos_s, (tm_s // ls, 1)), jnp.tile(sin_s, (tm_s // ls, 1))
            pc_p, qlat_p, qpe_p, ckv_p, kpe_p, kcat_p = _odd_in(xp, cos_p, sin_p, tm_p, lp // tm_p, in_arrs, BF16)
            pc_s, qlat_s, qpe_s, ckv_s, kpe_s, kcat_s = _odd_in(xs, cos_s, sin_s, tm_s, 1, in_arrs, F32)

            z64 = jnp.zeros((C_DECAY_LORA, C_WIDTH), F32)
            wa2 = jnp.concatenate([jnp.concatenate([rwkv_w2[i], z64], axis=1),
                                   jnp.concatenate([z64, rwkv_a2[i]], axis=1)], axis=0)
            row = lambda v: v.reshape(1, -1)
            rw_arrs = [row(rwkv_mu[i]), row(rwkv_w0[i]), wa2.astype(BF16), row(rwkv_a0[i]), rwkv_g2[i].astype(BF16),
                       row(rwkv_kk[i]), row(rwkv_ka[i]), row(rwkv_rk[i]), row(rwkv_gn_g[i]), row(rwkv_gn_b[i])]
            npair = C_WIDTH // LANES
            yc_p, st_p = _rwkv(pc_p, jnp.zeros((bp, 1, C_COLS), F32), jnp.zeros((bp, npair, LANES, LANES), F32),
                               bp, lp // RWKV_CHUNK, RWKV_CHUNK, rw_arrs)
            yc_s, st_s = _rwkv(pc_s, state_shift[i], _state_to_pairs(state_rwkv[i]), bs, 1, ls, rw_arrs)

            tq = 128
            ol_p = _attn_prompt(qlat_p, qpe_p, kcat_p, bp, lp, tq, _tile_rows(lp, 512))
            sub = LANES // D_ROPE
            kpe_rows = cache_kpe[i].reshape(-1, PAGE_SIZE // sub, LANES)
            ol_s = _attn_paged(page_table, qlat_s.reshape(bs, ls, -1), qpe_s.reshape(bs, ls, -1),
                               kcat_s.reshape(bs, ls, -1), cache_ckv[i], kpe_rows, math.gcd(n_pages, 16))

            out_arrs = [wuv_bd.astype(BF16), w_out_odd[i].astype(BF16), g1.reshape(1, -1), b1.reshape(1, -1)]
            xp_new = _odd_out(xp, yc_p, ol_p, tm_p, out_arrs)
            xs_new = _odd_out(xs, yc_s, ol_s.reshape(rows_s, -1), tm_s, out_arrs)

            outs.setdefault("rwkv_p", []).append(_pairs_to_state(st_p))
            outs.setdefault("rwkv_s", []).append(_pairs_to_state(st_s))
            outs.setdefault("shift_p", []).append(pc_p.reshape(bp, lp, C_COLS)[:, -1:])
            outs.setdefault("shift_s", []).append(pc_s.reshape(bs, ls, C_COLS)[:, -1:])
            outs.setdefault("ckv_p", []).append(ckv_p.reshape(bp, lp, KV_LORA))
            outs.setdefault("ckv_s", []).append(ckv_s.reshape(bs, ls, KV_LORA))
            outs.setdefault("kpe_p", []).append(kpe_p.reshape(bp, lp, D_ROPE))
            outs.setdefault("kpe_s", []).append(kpe_s.reshape(bs, ls, D_ROPE))
            xp, xs = xp_new, xs_new
        g2, b2 = ln_g[layer, 1], ln_b[layer, 1]
        xp = _moe(xp, tm_moe_p, wr_pad, rb_pad, wg[layer], wu[layer], wd[layer], g2, b2)
        xs = _moe(xs, tm_moe_s, wr_pad, rb_pad, wg[layer], wu[layer], wd[layer], g2, b2)

    st = lambda k: jnp.stack(outs[k])
    return (xp.reshape(bp, lp, D_MODEL), xs.reshape(bs, ls, D_MODEL), st("pool_p"), st("pool_s"), st("sgu_s"),
            st("rwkv_p"), st("rwkv_s"), st("shift_p"), st("shift_s"), st("ckv_p"), st("ckv_s"), st("kpe_p"),
            st("kpe_s"))
```

```python
import functools
import math

import jax
import jax.numpy as jnp
import numpy as np
from jax import lax
from jax.experimental import pallas as pl
from jax.experimental.pallas import tpu as pltpu

F32 = jnp.float32
BF16 = jnp.bfloat16

D_MODEL = 1024
DEPTH = 2
ALPHA = (2 * DEPTH) ** 0.25
LN_EPS = 1e-5
RMS_EPS = 1e-6
A_WIDTH = 512
POOL_WINDOWS = (2, 4, 8, 16)
HALO = 16
POOL_BUF = 15
B_WIDTH = 512
B_HEADS = 4
SGU_CHUNK = 128
C_WIDTH = 512
C_HEAD_DIM = 64
C_HEADS = 8
C_COLS = 1792
RWKV_GN_EPS = 64e-5
C_DECAY_LORA = 64
RWKV_CHUNK = 64
RWKV_BLOCK = 2
RWKV_CHUNK_SHORT = 16
RWKV_GROUP = 4
D_HEADS = 8
D_NOPE = 64
D_ROPE = 32
Q_LORA = 256
KV_LORA = 128
MLA_SCALE = (D_NOPE + D_ROPE) ** -0.5
ROPE_BASE = 10000.0
ONES_LANE = D_ROPE
EXP2_SCALE = MLA_SCALE * math.log2(math.e)
PAGE_SIZE = 128
N_EXPERTS = 16
D_EXPERT = 256
LANES = 128
VMEM_LIMIT = 56 * 1024 * 1024

NEG_INF = float("-inf")


def _cparams(sem):
    return pltpu.CompilerParams(dimension_semantics=sem, vmem_limit_bytes=VMEM_LIMIT)


def _dot(a, b):
    return jnp.dot(a.astype(BF16), b.astype(BF16), preferred_element_type=F32)


def _dot_nt(a, b):
    return lax.dot_general(a.astype(BF16), b.astype(BF16), (((1,), (1,)), ((), ())), preferred_element_type=F32)


def _dot_tn(a, b):
    return lax.dot_general(a.astype(BF16), b.astype(BF16), (((0,), (0,)), ((), ())), preferred_element_type=F32)


def _split3(a):
    a1 = a.astype(BF16)
    r1 = a - a1.astype(F32)
    a2 = r1.astype(BF16)
    a3 = (r1 - a2.astype(F32)).astype(BF16)
    return a1, a2, a3


def _split2(a):
    hi = a.astype(BF16)
    return hi, (a - hi.astype(F32)).astype(BF16)


def _dot3(a, b_hi, b_lo):
    a_hi, a_lo = _split2(a)
    return (jnp.dot(a_hi, b_hi, preferred_element_type=F32)
            + (jnp.dot(a_hi, b_lo, preferred_element_type=F32) + jnp.dot(a_lo, b_hi, preferred_element_type=F32)))


def _layernorm(r, g, b):
    mu = jnp.mean(r, axis=-1, keepdims=True)
    d = r - mu
    var = jnp.mean(d * d, axis=-1, keepdims=True)
    return d * lax.rsqrt(var + LN_EPS) * g + b


def _rmsnorm(x, g):
    return x * lax.rsqrt(jnp.mean(x * x, axis=-1, keepdims=True) + RMS_EPS) * g


def _gelu(x):
    return 0.5 * x * (1.0 + lax.erf(x * np.float32(math.sqrt(0.5))))


def _even_tail(x, z, pooled, wrefs, tm):
    pw_ref, ps_ref, sg_ref, sb_ref, sw_ref, sbc_ref, woh_ref, wol_ref, lg_ref, lb_ref = wrefs
    u, v = z[:, :B_WIDTH], z[:, B_WIDTH:]
    a_out = jnp.concatenate(
        [_dot3(pooled[g], *_split2(pw_ref[g])) for g in range(len(POOL_WINDOWS))], axis=1) * ps_ref[...]
    row = lax.broadcasted_iota(jnp.int32, (SGU_CHUNK, SGU_CHUNK), 0)
    col = lax.broadcasted_iota(jnp.int32, (SGU_CHUNK, SGU_CHUNK), 1)
    vn_parts, b_parts = [], []
    for h in range(B_HEADS):
        sl = slice(h * LANES, (h + 1) * LANES)
        vh = v[:, sl]
        mu = jnp.mean(vh, axis=-1, keepdims=True)
        d = vh - mu
        var = jnp.mean(d * d, axis=-1, keepdims=True)
        vn = d * lax.rsqrt(var + LN_EPS) * sg_ref[:, sl] + sb_ref[:, sl]
        vn_parts.append(vn)
        wm = jnp.where(col <= row, sw_ref[h], 0.0)
        bias = sbc_ref[:, h:h + 1]
        v_hi, v_lo = _split2(vn)
        mix = jnp.concatenate(
            [_dot3(wm, v_hi[n * SGU_CHUNK:(n + 1) * SGU_CHUNK], v_lo[n * SGU_CHUNK:(n + 1) * SGU_CHUNK]) + bias
             for n in range(tm // SGU_CHUNK)], axis=0)
        b_parts.append(u[:, sl] * mix)
    vn_all = jnp.concatenate(vn_parts, axis=1)
    cat = jnp.concatenate([a_out] + b_parts, axis=1)
    y = _dot3(cat, woh_ref[...], wol_ref[...])
    return _layernorm(ALPHA * x + y, lg_ref[...], lb_ref[...]), vn_all


def _even_prompt_kernel(x_ref, wih_ref, wil_ref, *rest, tm):
    wrefs, (xo_ref, tail_ref, abuf) = rest[:-3], rest[-3:]
    t = pl.program_id(1)
    x = x_ref[...]
    p = _dot3(x, wih_ref[...], wil_ref[...])
    a = p[:, :A_WIDTH]
    z = _gelu(p[:, A_WIDTH:])

    @pl.when(t == 0)
    def _():
        abuf[0:HALO, :] = jnp.zeros((HALO, A_WIDTH), F32)

    abuf[HALO:HALO + tm, :] = a
    pos = t * tm + lax.broadcasted_iota(jnp.int32, (tm, LANES), 0)
    pooled = []
    for g, w in enumerate(POOL_WINDOWS):
        sl = slice(g * LANES, (g + 1) * LANES)
        acc = a[:, sl]
        for s in range(1, w):
            acc = acc + abuf[HALO - s:HALO - s + tm, sl]
        cnt = jnp.minimum(w, pos + 1).astype(F32)
        pooled.append(acc / cnt - a[:, sl])
    tail = abuf[tm:tm + HALO, :]
    abuf[0:HALO, :] = tail
    tail_ref[...] = tail
    xo, _ = _even_tail(x, z, pooled, wrefs, tm)
    xo_ref[...] = xo


def _even_sample_kernel(x_ref, buf_ref, wih_ref, wil_ref, *rest, tm, ls):
    wrefs, (xo_ref, anew_ref, vn_ref, ext) = rest[:-4], rest[-4:]
    nseq = tm // ls
    x = x_ref[...]
    p = _dot3(x, wih_ref[...], wil_ref[...])
    a = p[:, :A_WIDTH]
    z = _gelu(p[:, A_WIDTH:])
    ext[:, 0:HALO, :] = buf_ref[...]
    ext[:, HALO:HALO + ls, :] = a.reshape(nseq, ls, A_WIDTH)
    pooled = []
    for g, w in enumerate(POOL_WINDOWS):
        sl = slice(g * LANES, (g + 1) * LANES)
        acc = ext[:, HALO:HALO + ls, sl]
        for s in range(1, w):
            acc = acc + ext[:, HALO - s:HALO - s + ls, sl]
        pooled.append(acc.reshape(tm, LANES) * np.float32(1.0 / w) - a[:, sl])
    xo, vn = _even_tail(x, z, pooled, wrefs, tm)
    xo_ref[...] = xo
    anew_ref[...] = a
    vn_ref[...] = vn


def _const_spec(shape):
    nd = len(shape)
    return pl.BlockSpec(shape, lambda *_: (0,) * nd)


def _split_weight_kernel(w_ref, hi_ref, lo_ref):
    hi, lo = _split2(w_ref[...])
    hi_ref[...] = hi
    lo_ref[...] = lo


def _split_weight(w):
    rows, cols = w.shape
    tr = _tile_rows(rows, 256)
    spec = pl.BlockSpec((tr, cols), lambda i: (i, 0))
    return pl.pallas_call(
        _split_weight_kernel,
        grid=(rows // tr,),
        in_specs=[spec],
        out_specs=[spec, spec],
        out_shape=[jax.ShapeDtypeStruct(w.shape, BF16)] * 2,
        compiler_params=_cparams(("arbitrary",)),
        name="split_weight",
    )(w)


def _even_weights(w_in_split, pool_w, pool_scale, sgu_g, sgu_b, sgu_w, sgu_bcol, w_out_split, ln_g, ln_b):
    arrs = [*w_in_split, pool_w, pool_scale.reshape(1, -1), sgu_g.reshape(1, -1),
            sgu_b.reshape(1, -1), sgu_w, sgu_bcol, *w_out_split, ln_g.reshape(1, -1), ln_b.reshape(1, -1)]
    return arrs, [_const_spec(a.shape) for a in arrs]


def _even_prompt(x2d, nb, seq, tm, wts):
    nt = seq // tm
    arrs, specs = wts
    row_spec = lambda w: pl.BlockSpec((tm, w), lambda b, t: (b * nt + t, 0))
    return pl.pallas_call(
        functools.partial(_even_prompt_kernel, tm=tm),
        grid=(nb, nt),
        in_specs=[row_spec(D_MODEL)] + specs,
        out_specs=[row_spec(D_MODEL), pl.BlockSpec((HALO, A_WIDTH), lambda b, t: (b, 0))],
        out_shape=[jax.ShapeDtypeStruct((nb * seq, D_MODEL), F32), jax.ShapeDtypeStruct((nb * HALO, A_WIDTH), F32)],
        scratch_shapes=[pltpu.VMEM((tm + HALO, A_WIDTH), F32)],
        compiler_params=_cparams(("arbitrary", "arbitrary")),
        name="even_prompt",
    )(x2d, *arrs)


def _even_sample(x2d, bufpad, ls, tm, wts):
    rows = x2d.shape[0]
    nseq = tm // ls
    arrs, specs = wts
    row_spec = lambda w: pl.BlockSpec((tm, w), lambda i: (i, 0))
    return pl.pallas_call(
        functools.partial(_even_sample_kernel, tm=tm, ls=ls),
        grid=(rows // tm,),
        in_specs=[row_spec(D_MODEL), pl.BlockSpec((nseq, HALO, A_WIDTH), lambda i: (i, 0, 0))] + specs,
        out_specs=[row_spec(D_MODEL), row_spec(A_WIDTH), row_spec(B_WIDTH)],
        out_shape=[jax.ShapeDtypeStruct((rows, D_MODEL), F32), jax.ShapeDtypeStruct((rows, A_WIDTH), F32),
                   jax.ShapeDtypeStruct((rows, B_WIDTH), F32)],
        scratch_shapes=[pltpu.VMEM((nseq, HALO + ls, A_WIDTH), F32)],
        compiler_params=_cparams(("arbitrary",)),
        name="even_sample",
    )(x2d, bufpad, *arrs)


def _route(scores, bias):
    shape = scores.shape
    lane = lax.broadcasted_iota(jnp.int32, shape, 1)
    pos_in_group = lane % 4
    group = lane // 4
    biased = scores + bias

    def from_lane(x, d):
        return pltpu.roll(x, d % LANES, 1)

    offsets = (-3, -2, -1, 1, 2, 3)
    rank = jnp.zeros(shape, jnp.int32)
    for d in offsets:
        src = pos_in_group - d
        valid = (src >= 0) & (src <= 3)
        other = from_lane(biased, d)
        beats = (other > biased) | ((other == biased) & (d > 0))
        rank = rank + jnp.where(valid & beats, 1, 0)
    top2 = rank < 2
    kept = jnp.where(top2, biased, 0.0)
    gsum = kept
    for d in offsets:
        src = pos_in_group - d
        valid = (src >= 0) & (src <= 3)
        gsum = gsum + jnp.where(valid, from_lane(kept, d), 0.0)
    lost = jnp.zeros(shape, jnp.int32)
    for dg in offsets:
        src = group - dg
        valid = (src >= 0) & (src <= 3)
        other = from_lane(gsum, 4 * dg)
        beats = (other > gsum) | ((other == gsum) & (dg > 0))
        lost = lost + jnp.where(valid & beats, 1, 0)
    selected = top2 & (lost == 0) & (lane < N_EXPERTS)
    picked = jnp.where(selected, scores, 0.0)
    return picked / jnp.sum(picked, axis=-1, keepdims=True)


def _moe_kernel(x_ref, wr_ref, rb_ref, wg_ref, wu_ref, wd_ref, lg_ref, lb_ref, o_ref, hbuf):
    x = x_ref[...]
    x1 = x.astype(BF16)
    x2 = (x - x1.astype(F32)).astype(BF16)
    wr = wr_ref[...]
    w1 = wr.astype(BF16)
    w2 = (wr - w1.astype(F32)).astype(BF16)
    logits = (jnp.dot(x1, w1, preferred_element_type=F32)
              + (jnp.dot(x1, w2, preferred_element_type=F32) + jnp.dot(x2, w1, preferred_element_type=F32)))
    gates = _route(jax.nn.sigmoid(logits), rb_ref[...])
    for e in range(N_EXPERTS):
        hg = jnp.dot(x1, wg_ref[e], preferred_element_type=F32)
        hu = jnp.dot(x1, wu_ref[e], preferred_element_type=F32)
        he = hg * jax.nn.sigmoid(hg) * hu * gates[:, e:e + 1]
        hbuf[:, e * D_EXPERT:(e + 1) * D_EXPERT] = he.astype(BF16)
    y = jnp.dot(hbuf[...], wd_ref[...], preferred_element_type=F32)
    o_ref[...] = _layernorm(ALPHA * x + y, lg_ref[...], lb_ref[...])


def _resident_spec(shape):
    nd = len(shape)
    return pl.BlockSpec(shape, lambda *_: (0,) * nd, pipeline_mode=pl.Buffered(1))


def _moe(x2d, tm, wr_pad, rb_pad, wg, wu, wd, ln_g, ln_b):
    rows = x2d.shape[0]
    row_spec = pl.BlockSpec((tm, D_MODEL), lambda i: (i, 0))
    wd_all = wd.reshape(N_EXPERTS * D_EXPERT, D_MODEL)
    return pl.pallas_call(
        _moe_kernel,
        grid=(rows // tm,),
        in_specs=[row_spec, _const_spec(wr_pad.shape), _const_spec(rb_pad.shape),
                  _resident_spec(wg.shape), _resident_spec(wu.shape), _resident_spec(wd_all.shape),
                  _const_spec((1, D_MODEL)), _const_spec((1, D_MODEL))],
        out_specs=row_spec,
        out_shape=jax.ShapeDtypeStruct((rows, D_MODEL), F32),
        scratch_shapes=[pltpu.VMEM((tm, N_EXPERTS * D_EXPERT), BF16)],
        compiler_params=_cparams(("arbitrary",)),
        name="moe",
    )(x2d, wr_pad, rb_pad, wg, wu, wd_all, ln_g.reshape(1, -1), ln_b.reshape(1, -1))


def _rope_lanes(x, cos_t, sin_t):
    lane = lax.broadcasted_iota(jnp.int32, x.shape, 1)
    partner = jnp.where((lane % D_ROPE) < D_ROPE // 2,
                        pltpu.roll(x, LANES - D_ROPE // 2, 1), pltpu.roll(x, D_ROPE // 2, 1))
    return x * cos_t + partner * sin_t


def _key_tail(kpe_lanes):
    lane = lax.broadcasted_iota(jnp.int32, kpe_lanes.shape, 1)
    return jnp.where(lane < D_ROPE, kpe_lanes, jnp.where(lane == ONES_LANE, 1.0, 0.0))


def _odd_in_kernel(x_ref, cos_ref, sin_ref, wi_ref, gq_ref, gkv_ref, wuq_ref, wuk_ref,
                   pc_ref, qcat_ref, ckv_ref, kpe_ref, kcat_ref):
    p = _dot(x_ref[...], wi_ref[...])
    pc_ref[...] = p[:, :C_COLS]
    o = C_COLS
    cq = _rmsnorm(p[:, o:o + Q_LORA], gq_ref[...])
    ckv = _rmsnorm(p[:, o + Q_LORA:o + Q_LORA + KV_LORA], gkv_ref[...])
    cos_t, sin_t = cos_ref[...], sin_ref[...]
    kslot = _rope_lanes(p[:, o + Q_LORA + KV_LORA:o + Q_LORA + KV_LORA + LANES], cos_t, sin_t)
    q = _dot(cq, wuq_ref[...])
    nope = D_HEADS * D_NOPE
    qlat = _dot(q[:, :nope], wuk_ref[...])
    lane = lax.broadcasted_iota(jnp.int32, kslot.shape, 1)
    per_vreg = LANES // D_ROPE
    for i in range(D_HEADS // per_vreg):
        qpe = _rope_lanes(q[:, nope + i * LANES:nope + (i + 1) * LANES], cos_t, sin_t)
        for j in range(per_vreg):
            h = i * per_vreg + j
            pe = qpe if j == 0 else pltpu.roll(qpe, LANES - j * D_ROPE, 1)
            row = jnp.concatenate([qlat[:, h * KV_LORA:(h + 1) * KV_LORA], jnp.where(lane < D_ROPE, pe, 0.0)], axis=1)
            qcat_ref[:, h * 2 * LANES:(h + 1) * 2 * LANES] = row.astype(qcat_ref.dtype)
    ckv_ref[...] = ckv
    kpe_ref[...] = kslot[:, :D_ROPE]
    kcat_ref[...] = jnp.concatenate([ckv, _key_tail(kslot)], axis=1).astype(kcat_ref.dtype)


def _odd_in(x2d, cos_t, sin_t, tm, pos_blocks, arrs, qdt):
    rows = x2d.shape[0]
    row_spec = lambda w: pl.BlockSpec((tm, w), lambda i: (i, 0))
    pos_spec = pl.BlockSpec((tm, LANES), lambda i: (i % pos_blocks, 0))
    out_w = [(C_COLS, F32), (D_HEADS * 2 * LANES, qdt), (KV_LORA, F32), (D_ROPE, F32), (2 * LANES, qdt)]
    return pl.pallas_call(
        _odd_in_kernel,
        grid=(rows // tm,),
        in_specs=[row_spec(D_MODEL), pos_spec, pos_spec] + [_const_spec(a.shape) for a in arrs],
        out_specs=[row_spec(w) for w, _ in out_w],
        out_shape=[jax.ShapeDtypeStruct((rows, w), dt) for w, dt in out_w],
        compiler_params=_cparams(("arbitrary",)),
        name="odd_in",
    )(x2d, cos_t, sin_t, *arrs)


def _rwkv_kernel(pc_ref, shift_ref, p0_ref, mu_ref, w0_ref, wa2_ref, a0_ref, g2_ref, kkc_ref, kac_ref, rk_ref,
                 gng_ref, gnb_ref, ho_ref, y_ref, pout_ref, state, xbuf, *, chunk, n_blk, n_valid):
    C = chunk
    HG = RWKV_GROUP
    GR, GL = HG * C, HG * C_HEAD_DIM
    R = n_blk * C
    n_in = R if n_valid == C else n_valid
    n = pl.program_id(1)

    @pl.when(n == 0)
    def _():
        state[...] = p0_ref[0]
        xbuf[7:8, :] = shift_ref[0]

    pc = pc_ref[...].reshape(n_in, C_COLS)
    if n_in < R:
        pc = jnp.concatenate([pc, jnp.zeros((R - n_in, C_COLS), F32)], axis=0)
    xbuf[8:8 + R, :] = pc
    prev = xbuf[7:7 + R, :]
    xs = pc + (prev - pc) * mu_ref[...]
    xbuf[7:8, :] = pc[n_in - 1:n_in, :]

    W = C_WIDTH
    r, k, v = xs[:, :W], xs[:, W:2 * W], xs[:, 2 * W:3 * W]
    wa = xs[:, 3 * W:3 * W + LANES]
    gl = xs[:, 3 * W + LANES:]
    first_half_r = lax.broadcasted_iota(jnp.int32, (R, LANES), 1) < C_HEAD_DIM
    lora = _dot(jnp.where(first_half_r, jnp.tanh(wa), wa), wa2_ref[...])
    zw = -(w0_ref[...] + lora[:, :W])
    softplus = jnp.maximum(zw, 0.0) + jnp.log1p(jnp.exp(-jnp.abs(zw)))
    ld = -jnp.exp(-softplus - 0.5)
    a = jax.nn.sigmoid(a0_ref[...] + lora[:, W:])
    g = _dot(jax.nn.sigmoid(gl), g2_ref[...])

    head_ones = ho_ref[...]

    def head_sum(xv):
        return _dot(xv, head_ones)

    kk = k * kkc_ref[...]
    kp = k * (1.0 + (a - 1.0) * kac_ref[...])
    sums = head_sum(jnp.concatenate([kk * kk, r * kp * rk_ref[...]], axis=0))
    kk = kk * lax.rsqrt(sums[:R] + 1e-12)
    bonus = sums[R:] * v
    if n_in < R:
        valid = lax.broadcasted_iota(jnp.int32, (R, W), 0) < n_in
        ld = jnp.where(valid, ld, 0.0)
        kk = jnp.where(valid, kk, 0.0)
        kp = jnp.where(valid, kp, 0.0)
        v = jnp.where(valid, v, 0.0)
    b = kk * a

    tr = lax.broadcasted_iota(jnp.int32, (R, R), 0)
    tc = lax.broadcasted_iota(jnp.int32, (R, R), 1)
    tri = jnp.where(((tr // C) == (tc // C)) & (tc <= tr), 1.0, 0.0).astype(BF16)
    lam3 = jnp.dot(tri, jnp.concatenate(_split3(ld), axis=1), preferred_element_type=F32)
    lam = lam3[:, :W] + lam3[:, W:2 * W] + lam3[:, 2 * W:]
    lam_ends = [lam[c * C + C - 1:c * C + C, :] for c in range(n_blk)]
    lam_end = jnp.concatenate([jnp.broadcast_to(le, (C, W)) for le in lam_ends], axis=0)
    e_neg = jnp.exp(-lam)
    e_end = jnp.exp(lam_end - lam)
    rt = r * jnp.exp(lam)
    kt = kp * e_neg
    kkt = kk * jnp.exp(lam - ld)
    bt = b * e_neg
    kg = kp * e_end
    bg = b * e_end

    r2 = lax.broadcasted_iota(jnp.int32, (GR, GR), 0)
    c2 = lax.broadcasted_iota(jnp.int32, (GR, GR), 1)
    same_blk = (r2 // C) == (c2 // C)
    strict = same_blk & ((c2 % C) < (r2 % C))
    incl = same_blk & ((c2 % C) <= (r2 % C))
    eye_r = jnp.where(r2 == c2, 1.0, 0.0)
    eye_l = (lax.broadcasted_iota(jnp.int32, (GL, GL), 0) == lax.broadcasted_iota(jnp.int32, (GL, GL), 1))
    head_of_lane = lax.broadcasted_iota(jnp.int32, (C, GL), 1) // C_HEAD_DIM

    def stack(xv, c, q):
        xq = xv[c * C:(c + 1) * C, q * GL:(q + 1) * GL]
        return jnp.concatenate([jnp.where(head_of_lane == h, xq, 0.0) for h in range(HG)], axis=0)

    def unstack(xz):
        out = xz[:C]
        for h in range(1, HG):
            out = out + xz[h * C:(h + 1) * C]
        return out

    ngroup = W // GL
    pre = []
    for c, q in [(c, q) for c in range(n_blk) for q in range(ngroup)]:
        gz, rz, bz, kz, vz = stack(kkt, c, q), stack(rt, c, q), stack(bt, c, q), stack(kt, c, q), stack(v, c, q)
        bgz, kgz = stack(bg, c, q), stack(kg, c, q)
        sc = _dot_nt(jnp.concatenate([gz, rz], axis=0), jnp.concatenate([bz, kz], axis=0))
        aab = jnp.where(strict, sc[:GR, :GR], 0.0)
        aak = jnp.where(strict, sc[:GR, GR:], 0.0)
        rrb = jnp.where(incl, sc[GR:, :GR], 0.0)
        rrk = jnp.where(incl, sc[GR:, GR:], 0.0)
        tinv = eye_r - aab
        xp = _dot(aab, aab)
        for _ in range(int(math.log2(C)) - 1):
            sq = _dot(xp, jnp.concatenate([tinv, xp], axis=1))
            tinv = tinv + sq[:, :GR]
            xp = sq[:, GR:]
        akv = _dot(aak, vz)
        wu = _dot(tinv, jnp.concatenate([gz, akv], axis=1))
        tn = _dot_tn(jnp.concatenate([bgz, kgz], axis=1), jnp.concatenate([wu, vz], axis=1))
        g_end_q = jnp.exp(lam_ends[c][:, q * GL:(q + 1) * GL])
        m_mat = jnp.where(eye_l, g_end_q, 0.0) - tn[:GL, :GL]
        n_mat = tn[GL:, 2 * GL:] - tn[:GL, GL:2 * GL]
        rhs = jnp.concatenate([wu, jnp.concatenate([jnp.zeros((GR, GL), F32), -vz], axis=1)], axis=0)
        rw = _dot(jnp.concatenate([rrb, rrk], axis=1), rhs)
        pre.append((jnp.concatenate([unstack(rz - rw[:, :GL]), m_mat], axis=0), -unstack(rw[:, GL:]), n_mat))

    y_rows = []
    for c in range(n_blk):
        ys = []
        for q in range(ngroup):
            qm_m, y0, n_mat = pre[c * ngroup + q]
            yp = _dot(qm_m, state[q])
            ys.append(yp[:C] + y0)
            state[q] = yp[C:] + n_mat
        y_rows.append(jnp.concatenate(ys, axis=1))
    y = jnp.concatenate(y_rows, axis=0)

    inv_n = np.float32(1.0 / C_HEAD_DIM)
    mean = head_sum(y) * inv_n
    d = y - mean
    var = head_sum(d * d) * inv_n
    yn = d * lax.rsqrt(var + RWKV_GN_EPS) * gng_ref[...] + gnb_ref[...]
    out = (yn + bonus) * g
    y_ref[...] = out[:n_in].reshape(y_ref.shape)
    pout_ref[0] = state[...]


def _rwkv(pc, shift_in, p0, nb, seq, C, n_blk, arrs):
    gl = RWKV_GROUP * C_HEAD_DIM
    ngroup = C_WIDTH // gl
    if seq >= C:
        rows_blk = C * n_blk
        assert seq % rows_blk == 0, (seq, C, n_blk)
        n_valid, n_steps = C, seq // rows_blk
        pc_in, pc_spec = pc, pl.BlockSpec((rows_blk, C_COLS), lambda b, n: (b * n_steps + n, 0))
        y_shape = (pc.shape[0], C_WIDTH)
        y_spec = pl.BlockSpec((rows_blk, C_WIDTH), lambda b, n: (b * n_steps + n, 0))
    else:
        assert n_blk == 1
        rows_blk, n_valid, n_steps = C, seq, 1
        pc_in = pc.reshape(nb, seq, C_COLS)
        pc_spec = pl.BlockSpec((1, seq, C_COLS), lambda b, n: (b, 0, 0))
        y_shape = (nb, seq, C_WIDTH)
        y_spec = pl.BlockSpec((1, seq, C_WIDTH), lambda b, n: (b, 0, 0))
    y, pout = pl.pallas_call(
        functools.partial(_rwkv_kernel, chunk=C, n_blk=n_blk, n_valid=n_valid),
        grid=(nb, n_steps),
        in_specs=[pc_spec, pl.BlockSpec((1, 1, C_COLS), lambda b, n: (b, 0, 0)),
                  pl.BlockSpec((1, ngroup, gl, gl), lambda b, n: (b, 0, 0, 0))]
                 + [_const_spec(a.shape) for a in arrs],
        out_specs=[y_spec, pl.BlockSpec((1, ngroup, gl, gl), lambda b, n: (b, 0, 0, 0))],
        out_shape=[jax.ShapeDtypeStruct(y_shape, F32), jax.ShapeDtypeStruct((nb, ngroup, gl, gl), F32)],
        scratch_shapes=[pltpu.VMEM((ngroup, gl, gl), F32), pltpu.VMEM((rows_blk + 8, C_COLS), F32)],
        compiler_params=_cparams(("arbitrary", "arbitrary")),
        name="rwkv7",
    )(pc_in, shift_in, p0, *arrs)
    return y.reshape(-1, C_WIDTH), pout


def _state_to_groups(s):
    nb, hg, n = s.shape[0], RWKV_GROUP, C_HEAD_DIM
    p = jnp.swapaxes(s, -1, -2).reshape(nb, C_HEADS // hg, hg, n, n)
    eye = jnp.eye(hg, dtype=s.dtype)
    return (eye[None, None, :, None, :, None] * p[:, :, :, :, None, :]).reshape(nb, C_HEADS // hg, hg * n, hg * n)


def _groups_to_state(p):
    nb, hg, n = p.shape[0], RWKV_GROUP, C_HEAD_DIM
    p6 = p.reshape(nb, C_HEADS // hg, hg, n, hg, n)
    s = jnp.stack([p6[:, :, h, :, h, :] for h in range(hg)], axis=2).reshape(nb, C_HEADS, n, n)
    return jnp.swapaxes(s, -1, -2)


def _softmax_step(s, kc, m_ref, acc_ref, idx):
    m_old = m_ref[idx]
    m_new = jnp.maximum(m_old, jnp.max(s, axis=-1, keepdims=True))
    p = jnp.exp2((s - m_new) * EXP2_SCALE)
    corr = jnp.exp2((m_old - m_new) * EXP2_SCALE)
    acc_ref[idx] = acc_ref[idx] * corr + jnp.dot(p.astype(BF16), kc, preferred_element_type=F32)
    m_ref[idx] = m_new


def _softmax_result(acc):
    return acc[:, :KV_LORA] / acc[:, KV_LORA + ONES_LANE:KV_LORA + ONES_LANE + 1]


def _attn_prompt_kernel(qi_ref, kj_ref, q_ref, k_ref, o_ref, m_s, acc, *, tq, tk):
    s_id = pl.program_id(1)
    qi, kj = qi_ref[s_id], kj_ref[s_id]
    last = (qi * tq + tq - 1) // tk

    @pl.when(kj == 0)
    def _():
        m_s[...] = jnp.full(m_s.shape, NEG_INF, F32)
        acc[...] = jnp.zeros(acc.shape, F32)

    def update(masked):
        kc = k_ref[...]
        if masked:
            qpos = qi * tq + lax.broadcasted_iota(jnp.int32, (tq, tk), 0)
            kpos = kj * tk + lax.broadcasted_iota(jnp.int32, (tq, tk), 1)
            visible = qpos >= kpos
        for h in range(D_HEADS):
            s = lax.dot_general(q_ref[:, h * 2 * LANES:(h + 1) * 2 * LANES], kc, (((1,), (1,)), ((), ())),
                                preferred_element_type=F32)
            if masked:
                s = jnp.where(visible, s, NEG_INF)
            _softmax_step(s, kc, m_s, acc, h)

    @pl.when(kj < last)
    def _():
        update(False)

    @pl.when(kj == last)
    def _():
        update(True)
        o_ref[...] = jnp.concatenate([_softmax_result(acc[h]) for h in range(D_HEADS)], axis=1).astype(BF16)


def _attn_prompt(qcat, kcat, nb, seq, tq, tk):
    nq, nk = seq // tq, seq // tk
    pairs = [(qi, kj) for qi in range(nq) for kj in range((qi * tq + tq - 1) // tk + 1)]
    qi_tbl = jnp.asarray([p[0] for p in pairs], jnp.int32)
    kj_tbl = jnp.asarray([p[1] for p in pairs], jnp.int32)
    grid_spec = pltpu.PrefetchScalarGridSpec(
        num_scalar_prefetch=2,
        grid=(nb, len(pairs)),
        in_specs=[pl.BlockSpec((tq, D_HEADS * 2 * LANES), lambda b, s, qt, kt: (b * nq + qt[s], 0)),
                  pl.BlockSpec((tk, 2 * LANES), lambda b, s, qt, kt: (b * nk + kt[s], 0))],
        out_specs=pl.BlockSpec((tq, D_HEADS * KV_LORA), lambda b, s, qt, kt: (b * nq + qt[s], 0)),
        scratch_shapes=[pltpu.VMEM((D_HEADS, tq, 1), F32), pltpu.VMEM((D_HEADS, tq, 2 * LANES), F32)])
    return pl.pallas_call(
        functools.partial(_attn_prompt_kernel, tq=tq, tk=tk),
        grid_spec=grid_spec,
        out_shape=jax.ShapeDtypeStruct((nb * seq, D_HEADS * KV_LORA), BF16),
        compiler_params=_cparams(("arbitrary", "arbitrary")),
        name="attn_prompt",
    )(qi_tbl, kj_tbl, qcat, kcat)


def _attn_paged_kernel(pt_ref, q_ref, knew_ref, ckv_hbm, kpe_hbm, o_ref, ckv_buf, kpe_buf, sems, q_s, m_s, l_s,
                       acc, *, ls, gp):
    b, j = pl.program_id(0), pl.program_id(1)
    nbatch, ngroups = pl.num_programs(0), pl.num_programs(1)
    step = b * ngroups + j
    slot = step % 2

    def page_copies(bb, jj, sl):
        out = []
        for g in range(gp):
            page = pt_ref[bb, jj * gp + g]
            rows_g = pl.ds(g * PAGE_SIZE, PAGE_SIZE)
            out.append(pltpu.make_async_copy(ckv_hbm.at[page], ckv_buf.at[sl, rows_g], sems.at[sl]))
            out.append(pltpu.make_async_copy(kpe_hbm.at[page], kpe_buf.at[sl, rows_g], sems.at[sl]))
        return out

    @pl.when(step == 0)
    def _():
        for c in page_copies(b, j, slot):
            c.start()

    @pl.when(step + 1 < nbatch * ngroups)
    def _():
        nxt = step + 1
        for c in page_copies(nxt // ngroups, nxt % ngroups, 1 - slot):
            c.start()

    rows = D_HEADS * ls

    @pl.when(j == 0)
    def _():
        qc = q_ref[0]
        q_s[...] = jnp.concatenate(
            [qc[:, h * 2 * LANES:(h + 1) * 2 * LANES] for h in range(D_HEADS)], axis=0).astype(BF16)
        m_s[...] = jnp.full(m_s.shape, NEG_INF, F32)
        l_s[...] = jnp.zeros(l_s.shape, F32)
        acc[...] = jnp.zeros(acc.shape, F32)

    for c in page_copies(b, j, slot):
        c.wait()

    nt = (((1,), (1,)), ((), ()))
    q_lat = q_s[:, :KV_LORA]
    q_pe = q_s[:, KV_LORA:KV_LORA + D_ROPE]

    def online(s, vals):
        m_old = m_s[...]
        m_new = jnp.maximum(m_old, jnp.max(s, axis=-1, keepdims=True))
        p = jnp.exp2((s - m_new) * EXP2_SCALE)
        corr = jnp.exp2((m_old - m_new) * EXP2_SCALE)
        l_s[...] = l_s[...] * corr + jnp.sum(p, axis=-1, keepdims=True)
        acc[...] = acc[...] * corr + jnp.dot(p.astype(BF16), vals, preferred_element_type=F32)
        m_s[...] = m_new

    ck = ckv_buf[slot].astype(BF16)
    kp = kpe_buf[slot].astype(BF16)
    online(lax.dot_general(q_lat, ck, nt, preferred_element_type=F32)
           + lax.dot_general(q_pe, kp, nt, preferred_element_type=F32), ck)

    @pl.when(j == ngroups - 1)
    def _():
        kn = knew_ref[0]
        pad = jnp.concatenate([kn, jnp.zeros((LANES - ls, 2 * LANES), F32)], axis=0).astype(BF16)
        sn = lax.dot_general(q_s[...], pad, nt, preferred_element_type=F32)
        tq_pos = lax.broadcasted_iota(jnp.int32, (rows, LANES), 0) % ls
        tk_pos = lax.broadcasted_iota(jnp.int32, (rows, LANES), 1)
        online(jnp.where(tq_pos >= tk_pos, sn, NEG_INF), pad[:, :KV_LORA])
        o = acc[...] / l_s[...]
        o_ref[0] = jnp.concatenate([o[h * ls:(h + 1) * ls] for h in range(D_HEADS)], axis=1)


def _attn_paged(page_table, qcat3, knew3, cache_ckv, cache_kpe, gp):
    nbatch, ls = qcat3.shape[0], qcat3.shape[1]
    n_pages = page_table.shape[1]
    rows = D_HEADS * ls
    grid_spec = pltpu.PrefetchScalarGridSpec(
        num_scalar_prefetch=1,
        grid=(nbatch, n_pages // gp),
        in_specs=[pl.BlockSpec((1, ls, D_HEADS * 2 * LANES), lambda b, j, pt: (b, 0, 0)),
                  pl.BlockSpec((1, ls, 2 * LANES), lambda b, j, pt: (b, 0, 0)),
                  pl.BlockSpec(memory_space=pl.ANY), pl.BlockSpec(memory_space=pl.ANY)],
        out_specs=pl.BlockSpec((1, ls, D_HEADS * KV_LORA), lambda b, j, pt: (b, 0, 0)),
        scratch_shapes=[pltpu.VMEM((2, gp * PAGE_SIZE, KV_LORA), F32), pltpu.VMEM((2, gp * PAGE_SIZE, D_ROPE), F32),
                        pltpu.SemaphoreType.DMA((2,)), pltpu.VMEM((rows, 2 * LANES), BF16),
                        pltpu.VMEM((rows, 1), F32), pltpu.VMEM((rows, 1), F32), pltpu.VMEM((rows, KV_LORA), F32)])
    return pl.pallas_call(
        functools.partial(_attn_paged_kernel, ls=ls, gp=gp),
        grid_spec=grid_spec,
        out_shape=jax.ShapeDtypeStruct((nbatch, ls, D_HEADS * KV_LORA), F32),
        compiler_params=_cparams(("arbitrary", "arbitrary")),
        name="attn_paged",
    )(page_table, qcat3, knew3, cache_ckv, cache_kpe)


def _odd_out_kernel(x_ref, yc_ref, ol_ref, wuv_ref, wo_ref, lg_ref, lb_ref, o_ref):
    yd = _dot(ol_ref[...], wuv_ref[...])
    y = _dot(jnp.concatenate([yc_ref[...], yd], axis=1), wo_ref[...])
    o_ref[...] = _layernorm(ALPHA * x_ref[...] + y, lg_ref[...], lb_ref[...])


def _odd_out(x2d, yc, olat, tm, arrs):
    rows = x2d.shape[0]
    row_spec = lambda w: pl.BlockSpec((tm, w), lambda i: (i, 0))
    return pl.pallas_call(
        _odd_out_kernel,
        grid=(rows // tm,),
        in_specs=[row_spec(D_MODEL), row_spec(C_WIDTH), row_spec(D_HEADS * KV_LORA)]
                 + [_const_spec(a.shape) for a in arrs],
        out_specs=row_spec(D_MODEL),
        out_shape=jax.ShapeDtypeStruct((rows, D_MODEL), F32),
        compiler_params=_cparams(("arbitrary",)),
        name="odd_out",
    )(x2d, yc, olat, *arrs)


def _block_diag(blocks):
    n, r, c = blocks.shape
    eye = jnp.eye(n, dtype=blocks.dtype)
    return (eye[:, None, :, None] * blocks[:, :, None, :]).reshape(n * r, n * c)


def _rope_tables(positions):
    half = D_ROPE // 2
    inv = ROPE_BASE ** (-jnp.arange(half, dtype=F32) / half)
    ang = positions[:, None] * inv
    cos, sin = jnp.cos(ang), jnp.sin(ang)
    reps = LANES // D_ROPE
    return jnp.tile(jnp.concatenate([cos, cos], axis=1), (1, reps)), jnp.tile(jnp.concatenate([-sin, sin], axis=1), (1, reps))


def _tile_rows(n, cap):
    t = min(n, cap)
    assert n % t == 0, (n, cap)
    return t


def kernel(x_prompt, x_sample, state_pool_buf, state_rwkv, state_shift, cache_ckv, cache_kpe, page_table, ln_g, ln_b, w_in_even, pool_w, pool_scale, sgu_ln_g, sgu_ln_b, sgu_w, sgu_b, w_out_even, w_in_odd, rwkv_mu, rwkv_w0, rwkv_w2, rwkv_a0, rwkv_a2, rwkv_g2, rwkv_kk, rwkv_ka, rwkv_rk, rwkv_gn_g, rwkv_gn_b, mla_gq, mla_gkv, mla_wuq, mla_wuk, mla_wuv, w_out_odd, w_router, router_bias, w_gate, w_up, w_down):
    bp, lp, _ = x_prompt.shape
    bs, ls, _ = x_sample.shape
    n_pages = page_table.shape[1]
    past_len = n_pages * PAGE_SIZE
    assert ls == 8 and lp % SGU_CHUNK == 0 and lp % RWKV_CHUNK == 0
    rows_p, rows_s = bp * lp, bs * ls
    tm_p = _tile_rows(lp, 512)
    tm_s = _tile_rows(rows_s, 512)
    tm_moe_p = _tile_rows(rows_p, 512)
    tm_moe_s = _tile_rows(rows_s, 512)

    xp = x_prompt.reshape(rows_p, D_MODEL)
    xs = x_sample.reshape(rows_s, D_MODEL)

    wr_pad = jnp.pad(w_router, ((0, 0), (0, LANES - N_EXPERTS)))
    rb_pad = jnp.pad(router_bias, (0, LANES - N_EXPERTS)).reshape(1, LANES)
    wg, wu, wd = w_gate.astype(BF16), w_up.astype(BF16), w_down.astype(BF16)

    outs = {}
    for layer in range(DEPTH):
        i = layer // 2
        g1, b1 = ln_g[layer, 0], ln_b[layer, 0]
        if layer % 2 == 0:
            wi_split, wo_split = _split_weight(w_in_even[i]), _split_weight(w_out_even[i])
            wts_p = _even_weights(wi_split, pool_w[i], pool_scale[i], sgu_ln_g[i], sgu_ln_b[i], sgu_w[i],
                                  sgu_b[i].T, wo_split, g1, b1)
            reps = SGU_CHUNK // ls
            w_small = jnp.stack([_block_diag(jnp.broadcast_to(sgu_w[i][h, :ls, :ls], (reps, ls, ls)))
                                 for h in range(B_HEADS)])
            b_small = jnp.tile(sgu_b[i][:, :ls], (1, reps)).T
            wts_s = _even_weights(wi_split, pool_w[i], pool_scale[i], sgu_ln_g[i], sgu_ln_b[i], w_small,
                                  b_small, wo_split, g1, b1)
            xp, tail = _even_prompt(xp, bp, lp, _tile_rows(lp, 256), wts_p)
            bufpad = jnp.pad(state_pool_buf[i], ((0, 0), (HALO - POOL_BUF, 0), (0, 0)))
            xs, a_new, vn_s = _even_sample(xs, bufpad, ls, tm_s, wts_s)
            outs.setdefault("pool_p", []).append(tail.reshape(bp, HALO, A_WIDTH)[:, HALO - POOL_BUF:])
            ext = jnp.concatenate([state_pool_buf[i], a_new.reshape(bs, ls, A_WIDTH)], axis=1)
            outs.setdefault("pool_s", []).append(ext[:, -POOL_BUF:])
            outs.setdefault("sgu_s", []).append(vn_s.reshape(bs, ls, B_WIDTH))
        else:
            nope = D_HEADS * D_NOPE
            wq_cols = mla_wuq[i].reshape(Q_LORA, D_HEADS, D_NOPE + D_ROPE)
            wuq_perm = jnp.concatenate([wq_cols[:, :, :D_NOPE].reshape(Q_LORA, nope),
                                        wq_cols[:, :, D_NOPE:].reshape(Q_LORA, D_HEADS * D_ROPE)], axis=1)
            wuk_bd = _block_diag(jnp.swapaxes(mla_wuk[i], 1, 2))
            wuv_bd = _block_diag(mla_wuv[i])
            w_in = w_in_odd[i]
            w_in_pad = jnp.concatenate([w_in[:, :C_COLS + Q_LORA + KV_LORA],
                                        jnp.tile(w_in[:, C_COLS + Q_LORA + KV_LORA:], (1, LANES // D_ROPE))], axis=1)
            in_arrs = [w_in_pad.astype(BF16), mla_gq[i].reshape(1, -1), mla_gkv[i].reshape(1, -1),
                       wuq_perm.astype(BF16), wuk_bd.astype(BF16)]
            cos_p, sin_p = _rope_tables(jnp.arange(lp, dtype=F32))
            cos_s, sin_s = _rope_tables(past_len + jnp.arange(ls, dtype=F32))
            cos_s, sin_s = jnp.tile(cos_s, (tm_s // ls, 1)), jnp.tile(sin_s, (tm_s // ls, 1))
            pc_p, qcat_p, ckv_p, kpe_p, kcat_p = _odd_in(xp, cos_p, sin_p, tm_p, lp // tm_p, in_arrs, BF16)
            pc_s, qcat_s, ckv_s, kpe_s, kcat_s = _odd_in(xs, cos_s, sin_s, tm_s, 1, in_arrs, F32)

            z64 = jnp.zeros((C_DECAY_LORA, C_WIDTH), F32)
            wa2 = jnp.concatenate([jnp.concatenate([rwkv_w2[i], z64], axis=1),
                                   jnp.concatenate([z64, rwkv_a2[i]], axis=1)], axis=0)
            row = lambda v: v.reshape(1, -1)
            rw_arrs = [row(rwkv_mu[i]), row(rwkv_w0[i]), wa2.astype(BF16), row(rwkv_a0[i]), rwkv_g2[i].astype(BF16),
                       row(rwkv_kk[i]), row(rwkv_ka[i]), row(rwkv_rk[i]), row(rwkv_gn_g[i]), row(rwkv_gn_b[i]),
                       _block_diag(jnp.ones((C_HEADS, C_HEAD_DIM, C_HEAD_DIM), BF16))]
            zero_state = jnp.zeros((bp, C_HEADS, C_HEAD_DIM, C_HEAD_DIM), F32)
            yc_p, st_p = _rwkv(pc_p, jnp.zeros((bp, 1, C_COLS), F32), _state_to_groups(zero_state),
                               bp, lp, RWKV_CHUNK, RWKV_BLOCK, rw_arrs)
            yc_s, st_s = _rwkv(pc_s, state_shift[i], _state_to_groups(state_rwkv[i]), bs, ls, RWKV_CHUNK_SHORT, 1,
                               rw_arrs)

            ol_p = _attn_prompt(qcat_p, kcat_p, bp, lp, _tile_rows(lp, 256), _tile_rows(lp, 512))
            ol_s = _attn_paged(page_table, qcat_s.reshape(bs, ls, -1), kcat_s.reshape(bs, ls, -1),
                               cache_ckv[i], cache_kpe[i], math.gcd(n_pages, 16))

            out_arrs = [wuv_bd.astype(BF16), w_out_odd[i].astype(BF16), g1.reshape(1, -1), b1.reshape(1, -1)]
            xp_new = _odd_out(xp, yc_p, ol_p, tm_p, out_arrs)
            xs_new = _odd_out(xs, yc_s, ol_s.reshape(rows_s, -1), tm_s, out_arrs)

            outs.setdefault("rwkv_p", []).append(_groups_to_state(st_p))
            outs.setdefault("rwkv_s", []).append(_groups_to_state(st_s))
            outs.setdefault("shift_p", []).append(pc_p.reshape(bp, lp, C_COLS)[:, -1:])
            outs.setdefault("shift_s", []).append(pc_s.reshape(bs, ls, C_COLS)[:, -1:])
            outs.setdefault("ckv_p", []).append(ckv_p.reshape(bp, lp, KV_LORA))
            outs.setdefault("ckv_s", []).append(ckv_s.reshape(bs, ls, KV_LORA))
            outs.setdefault("kpe_p", []).append(kpe_p.reshape(bp, lp, D_ROPE))
            outs.setdefault("kpe_s", []).append(kpe_s.reshape(bs, ls, D_ROPE))
            xp, xs = xp_new, xs_new
        g2, b2 = ln_g[layer, 1], ln_b[layer, 1]
        xp = _moe(xp, tm_moe_p, wr_pad, rb_pad, wg[layer], wu[layer], wd[layer], g2, b2)
        xs = _moe(xs, tm_moe_s, wr_pad, rb_pad, wg[layer], wu[layer], wd[layer], g2, b2)

    st = lambda k: jnp.stack(outs[k])
    return (xp.reshape(bp, lp, D_MODEL), xs.reshape(bs, ls, D_MODEL), st("pool_p"), st("pool_s"), st("sgu_s"),
            st("rwkv_p"), st("rwkv_s"), st("shift_p"), st("shift_s"), st("ckv_p"), st("ckv_s"), st("kpe_p"),
            st("kpe_s"))
```

```python
import functools
import math

import jax
import jax.numpy as jnp
import numpy as np
from jax import lax
from jax.experimental import pallas as pl
from jax.experimental.pallas import tpu as pltpu

F32 = jnp.float32
BF16 = jnp.bfloat16

D_MODEL = 1024
DEPTH = 2
ALPHA = (2 * DEPTH) ** 0.25
LN_EPS = 1e-5
RMS_EPS = 1e-6
A_WIDTH = 512
POOL_WINDOWS = (2, 4, 8, 16)
HALO = 16
POOL_BUF = 15
B_WIDTH = 512
B_HEADS = 4
SGU_CHUNK = 128
C_WIDTH = 512
C_HEAD_DIM = 64
C_HEADS = 8
C_COLS = 1792
RWKV_GN_EPS = 64e-5
C_DECAY_LORA = 64
RWKV_CHUNK = 64
RWKV_BLOCK = 4
RWKV_CHUNK_SHORT = 16
RWKV_GROUP = 4
D_HEADS = 8
D_NOPE = 64
D_ROPE = 32
Q_LORA = 256
KV_LORA = 128
MLA_SCALE = (D_NOPE + D_ROPE) ** -0.5
ROPE_BASE = 10000.0
ONES_LANE = D_ROPE
EXP2_SCALE = MLA_SCALE * math.log2(math.e)
PAGE_SIZE = 128
PAGES_PER_STEP = 64
N_EXPERTS = 16
D_EXPERT = 256
LANES = 128
VMEM_LIMIT = 56 * 1024 * 1024

NEG_INF = float("-inf")


def _cparams(sem):
    return pltpu.CompilerParams(dimension_semantics=sem, vmem_limit_bytes=VMEM_LIMIT)


def _dot(a, b):
    return jnp.dot(a.astype(BF16), b.astype(BF16), preferred_element_type=F32)


def _dot_nt(a, b):
    return lax.dot_general(a.astype(BF16), b.astype(BF16), (((1,), (1,)), ((), ())), preferred_element_type=F32)


def _dot_tn(a, b):
    return lax.dot_general(a.astype(BF16), b.astype(BF16), (((0,), (0,)), ((), ())), preferred_element_type=F32)


def _split3(a):
    a1 = a.astype(BF16)
    r1 = a - a1.astype(F32)
    a2 = r1.astype(BF16)
    a3 = (r1 - a2.astype(F32)).astype(BF16)
    return a1, a2, a3


def _split2(a):
    hi = a.astype(BF16)
    return hi, (a - hi.astype(F32)).astype(BF16)


def _dot3(a, b_hi, b_lo):
    a_hi, a_lo = _split2(a)
    return (jnp.dot(a_hi, b_hi, preferred_element_type=F32)
            + (jnp.dot(a_hi, b_lo, preferred_element_type=F32) + jnp.dot(a_lo, b_hi, preferred_element_type=F32)))


def _layernorm(r, g, b):
    mu = jnp.mean(r, axis=-1, keepdims=True)
    d = r - mu
    var = jnp.mean(d * d, axis=-1, keepdims=True)
    return d * lax.rsqrt(var + LN_EPS) * g + b


def _rmsnorm(x, g):
    return x * lax.rsqrt(jnp.mean(x * x, axis=-1, keepdims=True) + RMS_EPS) * g


def _gelu(x):
    return 0.5 * x * (1.0 + lax.erf(x * np.float32(math.sqrt(0.5))))


def _even_tail(x, z, pooled, wrefs, tm):
    pw_ref, ps_ref, sg_ref, sb_ref, sw_ref, sbc_ref, woh_ref, wol_ref, lg_ref, lb_ref = wrefs
    u, v = z[:, :B_WIDTH], z[:, B_WIDTH:]
    a_out = jnp.concatenate(
        [_dot3(pooled[g], *_split2(pw_ref[g])) for g in range(len(POOL_WINDOWS))], axis=1) * ps_ref[...]
    row = lax.broadcasted_iota(jnp.int32, (SGU_CHUNK, SGU_CHUNK), 0)
    col = lax.broadcasted_iota(jnp.int32, (SGU_CHUNK, SGU_CHUNK), 1)
    vn_parts, b_parts = [], []
    for h in range(B_HEADS):
        sl = slice(h * LANES, (h + 1) * LANES)
        vh = v[:, sl]
        mu = jnp.mean(vh, axis=-1, keepdims=True)
        d = vh - mu
        var = jnp.mean(d * d, axis=-1, keepdims=True)
        vn = d * lax.rsqrt(var + LN_EPS) * sg_ref[:, sl] + sb_ref[:, sl]
        vn_parts.append(vn)
        wm = jnp.where(col <= row, sw_ref[h], 0.0)
        bias = sbc_ref[:, h:h + 1]
        v_hi, v_lo = _split2(vn)
        mix = jnp.concatenate(
            [_dot3(wm, v_hi[n * SGU_CHUNK:(n + 1) * SGU_CHUNK], v_lo[n * SGU_CHUNK:(n + 1) * SGU_CHUNK]) + bias
             for n in range(tm // SGU_CHUNK)], axis=0)
        b_parts.append(u[:, sl] * mix)
    vn_all = jnp.concatenate(vn_parts, axis=1)
    cat = jnp.concatenate([a_out] + b_parts, axis=1)
    y = _dot3(cat, woh_ref[...], wol_ref[...])
    return _layernorm(ALPHA * x + y, lg_ref[...], lb_ref[...]), vn_all


def _even_prompt_kernel(x_ref, wih_ref, wil_ref, *rest, tm):
    wrefs, (xo_ref, tail_ref, abuf) = rest[:-3], rest[-3:]
    t = pl.program_id(1)
    x = x_ref[...]
    p = _dot3(x, wih_ref[...], wil_ref[...])
    a = p[:, :A_WIDTH]
    z = _gelu(p[:, A_WIDTH:])

    @pl.when(t == 0)
    def _():
        abuf[0:HALO, :] = jnp.zeros((HALO, A_WIDTH), F32)

    abuf[HALO:HALO + tm, :] = a
    pos = t * tm + lax.broadcasted_iota(jnp.int32, (tm, LANES), 0)
    pooled = []
    for g, w in enumerate(POOL_WINDOWS):
        sl = slice(g * LANES, (g + 1) * LANES)
        acc = a[:, sl]
        for s in range(1, w):
            acc = acc + abuf[HALO - s:HALO - s + tm, sl]
        cnt = jnp.minimum(w, pos + 1).astype(F32)
        pooled.append(acc / cnt - a[:, sl])
    tail = abuf[tm:tm + HALO, :]
    abuf[0:HALO, :] = tail
    tail_ref[...] = tail
    xo, _ = _even_tail(x, z, pooled, wrefs, tm)
    xo_ref[...] = xo


def _even_sample_kernel(x_ref, buf_ref, wih_ref, wil_ref, *rest, tm, ls):
    wrefs, (xo_ref, anew_ref, vn_ref, ext) = rest[:-4], rest[-4:]
    nseq = tm // ls
    x = x_ref[...]
    p = _dot3(x, wih_ref[...], wil_ref[...])
    a = p[:, :A_WIDTH]
    z = _gelu(p[:, A_WIDTH:])
    ext[:, 0:HALO, :] = buf_ref[...]
    ext[:, HALO:HALO + ls, :] = a.reshape(nseq, ls, A_WIDTH)
    pooled = []
    for g, w in enumerate(POOL_WINDOWS):
        sl = slice(g * LANES, (g + 1) * LANES)
        acc = ext[:, HALO:HALO + ls, sl]
        for s in range(1, w):
            acc = acc + ext[:, HALO - s:HALO - s + ls, sl]
        pooled.append(acc.reshape(tm, LANES) * np.float32(1.0 / w) - a[:, sl])
    xo, vn = _even_tail(x, z, pooled, wrefs, tm)
    xo_ref[...] = xo
    anew_ref[...] = a
    vn_ref[...] = vn


def _const_spec(shape):
    nd = len(shape)
    return pl.BlockSpec(shape, lambda *_: (0,) * nd)


def _split_weight_kernel(w_ref, hi_ref, lo_ref):
    hi, lo = _split2(w_ref[...])
    hi_ref[...] = hi
    lo_ref[...] = lo


def _split_weight(w):
    rows, cols = w.shape
    tr = _tile_rows(rows, 256)
    spec = pl.BlockSpec((tr, cols), lambda i: (i, 0))
    return pl.pallas_call(
        _split_weight_kernel,
        grid=(rows // tr,),
        in_specs=[spec],
        out_specs=[spec, spec],
        out_shape=[jax.ShapeDtypeStruct(w.shape, BF16)] * 2,
        compiler_params=_cparams(("arbitrary",)),
        name="split_weight",
    )(w)


def _even_weights(w_in_split, pool_w, pool_scale, sgu_g, sgu_b, sgu_w, sgu_bcol, w_out_split, ln_g, ln_b):
    arrs = [*w_in_split, pool_w, pool_scale.reshape(1, -1), sgu_g.reshape(1, -1),
            sgu_b.reshape(1, -1), sgu_w, sgu_bcol, *w_out_split, ln_g.reshape(1, -1), ln_b.reshape(1, -1)]
    return arrs, [_const_spec(a.shape) for a in arrs]


def _even_prompt(x2d, nb, seq, tm, wts):
    nt = seq // tm
    arrs, specs = wts
    row_spec = lambda w: pl.BlockSpec((tm, w), lambda b, t: (b * nt + t, 0))
    return pl.pallas_call(
        functools.partial(_even_prompt_kernel, tm=tm),
        grid=(nb, nt),
        in_specs=[row_spec(D_MODEL)] + specs,
        out_specs=[row_spec(D_MODEL), pl.BlockSpec((HALO, A_WIDTH), lambda b, t: (b, 0))],
        out_shape=[jax.ShapeDtypeStruct((nb * seq, D_MODEL), F32), jax.ShapeDtypeStruct((nb * HALO, A_WIDTH), F32)],
        scratch_shapes=[pltpu.VMEM((tm + HALO, A_WIDTH), F32)],
        compiler_params=_cparams(("arbitrary", "arbitrary")),
        name="even_prompt",
    )(x2d, *arrs)


def _even_sample(x2d, bufpad, ls, tm, wts):
    rows = x2d.shape[0]
    nseq = tm // ls
    arrs, specs = wts
    row_spec = lambda w: pl.BlockSpec((tm, w), lambda i: (i, 0))
    return pl.pallas_call(
        functools.partial(_even_sample_kernel, tm=tm, ls=ls),
        grid=(rows // tm,),
        in_specs=[row_spec(D_MODEL), pl.BlockSpec((nseq, HALO, A_WIDTH), lambda i: (i, 0, 0))] + specs,
        out_specs=[row_spec(D_MODEL), row_spec(A_WIDTH), row_spec(B_WIDTH)],
        out_shape=[jax.ShapeDtypeStruct((rows, D_MODEL), F32), jax.ShapeDtypeStruct((rows, A_WIDTH), F32),
                   jax.ShapeDtypeStruct((rows, B_WIDTH), F32)],
        scratch_shapes=[pltpu.VMEM((nseq, HALO + ls, A_WIDTH), F32)],
        compiler_params=_cparams(("arbitrary",)),
        name="even_sample",
    )(x2d, bufpad, *arrs)


def _route(scores, bias):
    shape = scores.shape
    lane = lax.broadcasted_iota(jnp.int32, shape, 1)
    pos_in_group = lane % 4
    group = lane // 4
    biased = scores + bias

    def from_lane(x, d):
        return pltpu.roll(x, d % LANES, 1)

    offsets = (-3, -2, -1, 1, 2, 3)
    rank = jnp.zeros(shape, jnp.int32)
    for d in offsets:
        src = pos_in_group - d
        valid = (src >= 0) & (src <= 3)
        other = from_lane(biased, d)
        beats = (other > biased) | ((other == biased) & (d > 0))
        rank = rank + jnp.where(valid & beats, 1, 0)
    top2 = rank < 2
    kept = jnp.where(top2, biased, 0.0)
    gsum = kept
    for d in offsets:
        src = pos_in_group - d
        valid = (src >= 0) & (src <= 3)
        gsum = gsum + jnp.where(valid, from_lane(kept, d), 0.0)
    lost = jnp.zeros(shape, jnp.int32)
    for dg in offsets:
        src = group - dg
        valid = (src >= 0) & (src <= 3)
        other = from_lane(gsum, 4 * dg)
        beats = (other > gsum) | ((other == gsum) & (dg > 0))
        lost = lost + jnp.where(valid & beats, 1, 0)
    selected = top2 & (lost == 0) & (lane < N_EXPERTS)
    picked = jnp.where(selected, scores, 0.0)
    return picked / jnp.sum(picked, axis=-1, keepdims=True)


def _moe_kernel(x_ref, wr_ref, rb_ref, wg_ref, wu_ref, wd_ref, lg_ref, lb_ref, o_ref, hbuf):
    x = x_ref[...]
    x1 = x.astype(BF16)
    x2 = (x - x1.astype(F32)).astype(BF16)
    wr = wr_ref[...]
    w1 = wr.astype(BF16)
    w2 = (wr - w1.astype(F32)).astype(BF16)
    logits = (jnp.dot(x1, w1, preferred_element_type=F32)
              + (jnp.dot(x1, w2, preferred_element_type=F32) + jnp.dot(x2, w1, preferred_element_type=F32)))
    gates = _route(jax.nn.sigmoid(logits), rb_ref[...])
    for e in range(N_EXPERTS):
        hg = jnp.dot(x1, wg_ref[e], preferred_element_type=F32)
        hu = jnp.dot(x1, wu_ref[e], preferred_element_type=F32)
        he = hg * jax.nn.sigmoid(hg) * hu * gates[:, e:e + 1]
        hbuf[:, e * D_EXPERT:(e + 1) * D_EXPERT] = he.astype(BF16)
    y = jnp.dot(hbuf[...], wd_ref[...], preferred_element_type=F32)
    o_ref[...] = _layernorm(ALPHA * x + y, lg_ref[...], lb_ref[...])


def _resident_spec(shape):
    nd = len(shape)
    return pl.BlockSpec(shape, lambda *_: (0,) * nd, pipeline_mode=pl.Buffered(1))


def _moe(x2d, tm, wr_pad, rb_pad, wg, wu, wd, ln_g, ln_b):
    rows = x2d.shape[0]
    row_spec = pl.BlockSpec((tm, D_MODEL), lambda i: (i, 0))
    wd_all = wd.reshape(N_EXPERTS * D_EXPERT, D_MODEL)
    return pl.pallas_call(
        _moe_kernel,
        grid=(rows // tm,),
        in_specs=[row_spec, _const_spec(wr_pad.shape), _const_spec(rb_pad.shape),
                  _resident_spec(wg.shape), _resident_spec(wu.shape), _resident_spec(wd_all.shape),
                  _const_spec((1, D_MODEL)), _const_spec((1, D_MODEL))],
        out_specs=row_spec,
        out_shape=jax.ShapeDtypeStruct((rows, D_MODEL), F32),
        scratch_shapes=[pltpu.VMEM((tm, N_EXPERTS * D_EXPERT), BF16)],
        compiler_params=_cparams(("arbitrary",)),
        name="moe",
    )(x2d, wr_pad, rb_pad, wg, wu, wd_all, ln_g.reshape(1, -1), ln_b.reshape(1, -1))


def _rope_lanes(x, cos_t, sin_t):
    lane = lax.broadcasted_iota(jnp.int32, x.shape, 1)
    partner = jnp.where((lane % D_ROPE) < D_ROPE // 2,
                        pltpu.roll(x, LANES - D_ROPE // 2, 1), pltpu.roll(x, D_ROPE // 2, 1))
    return x * cos_t + partner * sin_t


def _key_tail(kpe_lanes):
    lane = lax.broadcasted_iota(jnp.int32, kpe_lanes.shape, 1)
    return jnp.where(lane < D_ROPE, kpe_lanes, jnp.where(lane == ONES_LANE, 1.0, 0.0))


def _odd_in_kernel(x_ref, cos_ref, sin_ref, wi_ref, gq_ref, gkv_ref, wuq_ref, wuk_ref,
                   pc_ref, qcat_ref, ckv_ref, kpe_ref, kcat_ref):
    p = _dot(x_ref[...], wi_ref[...])
    pc_ref[...] = p[:, :C_COLS]
    o = C_COLS
    cq = _rmsnorm(p[:, o:o + Q_LORA], gq_ref[...])
    ckv = _rmsnorm(p[:, o + Q_LORA:o + Q_LORA + KV_LORA], gkv_ref[...])
    cos_t, sin_t = cos_ref[...], sin_ref[...]
    kslot = _rope_lanes(p[:, o + Q_LORA + KV_LORA:o + Q_LORA + KV_LORA + LANES], cos_t, sin_t)
    q = _dot(cq, wuq_ref[...])
    nope = D_HEADS * D_NOPE
    qlat = _dot(q[:, :nope], wuk_ref[...])
    lane = lax.broadcasted_iota(jnp.int32, kslot.shape, 1)
    per_vreg = LANES // D_ROPE
    for i in range(D_HEADS // per_vreg):
        qpe = _rope_lanes(q[:, nope + i * LANES:nope + (i + 1) * LANES], cos_t, sin_t)
        for j in range(per_vreg):
            h = i * per_vreg + j
            pe = qpe if j == 0 else pltpu.roll(qpe, LANES - j * D_ROPE, 1)
            row = jnp.concatenate([qlat[:, h * KV_LORA:(h + 1) * KV_LORA], jnp.where(lane < D_ROPE, pe, 0.0)], axis=1)
            qcat_ref[:, h * 2 * LANES:(h + 1) * 2 * LANES] = row.astype(qcat_ref.dtype)
    ckv_ref[...] = ckv
    kpe_ref[...] = kslot[:, :D_ROPE]
    kcat_ref[...] = jnp.concatenate([ckv, _key_tail(kslot)], axis=1).astype(kcat_ref.dtype)


def _odd_in(x2d, cos_t, sin_t, tm, pos_blocks, arrs, qdt):
    rows = x2d.shape[0]
    row_spec = lambda w: pl.BlockSpec((tm, w), lambda i: (i, 0))
    pos_spec = pl.BlockSpec((tm, LANES), lambda i: (i % pos_blocks, 0))
    out_w = [(C_COLS, F32), (D_HEADS * 2 * LANES, qdt), (KV_LORA, F32), (D_ROPE, F32), (2 * LANES, qdt)]
    return pl.pallas_call(
        _odd_in_kernel,
        grid=(rows // tm,),
        in_specs=[row_spec(D_MODEL), pos_spec, pos_spec] + [_const_spec(a.shape) for a in arrs],
        out_specs=[row_spec(w) for w, _ in out_w],
        out_shape=[jax.ShapeDtypeStruct((rows, w), dt) for w, dt in out_w],
        compiler_params=_cparams(("arbitrary",)),
        name="odd_in",
    )(x2d, cos_t, sin_t, *arrs)


def _rwkv_kernel(pc_ref, shift_ref, p0_ref, mu_ref, w0_ref, wa2_ref, a0_ref, g2_ref, kkc_ref, kac_ref, rk_ref,
                 gng_ref, gnb_ref, ho_ref, y_ref, pout_ref, state, xbuf, *, chunk, n_blk, n_valid):
    C = chunk
    HG = RWKV_GROUP
    GR, GL = HG * C, HG * C_HEAD_DIM
    R = n_blk * C
    n_in = R if n_valid == C else n_valid
    n = pl.program_id(1)

    @pl.when(n == 0)
    def _():
        state[...] = p0_ref[0]
        xbuf[7:8, :] = shift_ref[0]

    pc = pc_ref[...].reshape(n_in, C_COLS)
    if n_in < R:
        pc = jnp.concatenate([pc, jnp.zeros((R - n_in, C_COLS), F32)], axis=0)
    xbuf[8:8 + R, :] = pc
    prev = xbuf[7:7 + R, :]
    xs = pc + (prev - pc) * mu_ref[...]
    xbuf[7:8, :] = pc[n_in - 1:n_in, :]

    W = C_WIDTH
    r, k, v = xs[:, :W], xs[:, W:2 * W], xs[:, 2 * W:3 * W]
    wa = xs[:, 3 * W:3 * W + LANES]
    gl = xs[:, 3 * W + LANES:]
    first_half_r = lax.broadcasted_iota(jnp.int32, (R, LANES), 1) < C_HEAD_DIM
    lora = _dot(jnp.where(first_half_r, jnp.tanh(wa), wa), wa2_ref[...])
    zw = -(w0_ref[...] + lora[:, :W])
    softplus = jnp.maximum(zw, 0.0) + jnp.log1p(jnp.exp(-jnp.abs(zw)))
    ld = -jnp.exp(-softplus - 0.5)
    a = jax.nn.sigmoid(a0_ref[...] + lora[:, W:])
    g = _dot(jax.nn.sigmoid(gl), g2_ref[...])

    head_ones = ho_ref[...]

    def head_sum(xv):
        return _dot(xv, head_ones)

    kk = k * kkc_ref[...]
    kp = k * (1.0 + (a - 1.0) * kac_ref[...])
    sums = head_sum(jnp.concatenate([kk * kk, r * kp * rk_ref[...]], axis=0))
    kk = kk * lax.rsqrt(sums[:R] + 1e-12)
    bonus = sums[R:] * v
    if n_in < R:
        valid = lax.broadcasted_iota(jnp.int32, (R, W), 0) < n_in
        ld = jnp.where(valid, ld, 0.0)
        kk = jnp.where(valid, kk, 0.0)
        kp = jnp.where(valid, kp, 0.0)
        v = jnp.where(valid, v, 0.0)
    b = kk * a

    tr = lax.broadcasted_iota(jnp.int32, (R, R), 0)
    tc = lax.broadcasted_iota(jnp.int32, (R, R), 1)
    tri = jnp.where(((tr // C) == (tc // C)) & (tc <= tr), 1.0, 0.0).astype(BF16)
    lam3 = jnp.dot(tri, jnp.concatenate(_split3(ld), axis=1), preferred_element_type=F32)
    lam = lam3[:, :W] + lam3[:, W:2 * W] + lam3[:, 2 * W:]
    lam_ends = [lam[c * C + C - 1:c * C + C, :] for c in range(n_blk)]
    lam_end = jnp.concatenate([jnp.broadcast_to(le, (C, W)) for le in lam_ends], axis=0)
    e_neg = jnp.exp(-lam)
    e_end = jnp.exp(lam_end - lam)
    rt = r * jnp.exp(lam)
    kt = kp * e_neg
    kkt = kk * jnp.exp(lam - ld)
    bt = b * e_neg
    kg = kp * e_end
    bg = b * e_end

    r2 = lax.broadcasted_iota(jnp.int32, (GR, GR), 0)
    c2 = lax.broadcasted_iota(jnp.int32, (GR, GR), 1)
    same_blk = (r2 // C) == (c2 // C)
    strict = same_blk & ((c2 % C) < (r2 % C))
    incl = same_blk & ((c2 % C) <= (r2 % C))
    eye_r = jnp.where(r2 == c2, 1.0, 0.0)
    eye_l = (lax.broadcasted_iota(jnp.int32, (GL, GL), 0) == lax.broadcasted_iota(jnp.int32, (GL, GL), 1))
    head_of_lane = lax.broadcasted_iota(jnp.int32, (C, GL), 1) // C_HEAD_DIM

    def stack(xv, c, q):
        xq = xv[c * C:(c + 1) * C, q * GL:(q + 1) * GL]
        return jnp.concatenate([jnp.where(head_of_lane == h, xq, 0.0) for h in range(HG)], axis=0)

    def unstack(xz):
        out = xz[:C]
        for h in range(1, HG):
            out = out + xz[h * C:(h + 1) * C]
        return out

    ngroup = W // GL
    pre = []
    for c, q in [(c, q) for c in range(n_blk) for q in range(ngroup)]:
        gz, rz, bz, kz, vz = stack(kkt, c, q), stack(rt, c, q), stack(bt, c, q), stack(kt, c, q), stack(v, c, q)
        bgz, kgz = stack(bg, c, q), stack(kg, c, q)
        sc = _dot_nt(jnp.concatenate([gz, rz], axis=0), jnp.concatenate([bz, kz], axis=0))
        aab = jnp.where(strict, sc[:GR, :GR], 0.0)
        aak = jnp.where(strict, sc[:GR, GR:], 0.0)
        rrb = jnp.where(incl, sc[GR:, :GR], 0.0)
        rrk = jnp.where(incl, sc[GR:, GR:], 0.0)
        tinv = eye_r - aab
        xp = _dot(aab, aab)
        for _ in range(int(math.log2(C)) - 2):
            sq = _dot(xp, jnp.concatenate([tinv, xp], axis=1))
            tinv = tinv + sq[:, :GR]
            xp = sq[:, GR:]
        tinv = tinv + _dot(xp, tinv)
        akv = _dot(aak, vz)
        wu = _dot(tinv, jnp.concatenate([gz, akv], axis=1))
        bwu = _dot_tn(bgz, wu)
        g_end_q = jnp.exp(lam_ends[c][:, q * GL:(q + 1) * GL])
        m_mat = jnp.where(eye_l, g_end_q, 0.0) - bwu[:, :GL]
        n_mat = _dot_tn(kgz, vz) - bwu[:, GL:]
        rwu = _dot(rrb, wu)
        y0 = unstack(_dot(rrk, vz) - rwu[:, GL:])
        pre.append((jnp.concatenate([unstack(rz - rwu[:, :GL]), m_mat], axis=0), y0, n_mat))

    y_rows = []
    for c in range(n_blk):
        ys = []
        for q in range(ngroup):
            qm_m, y0, n_mat = pre[c * ngroup + q]
            yp = _dot(qm_m, state[q])
            ys.append(yp[:C] + y0)
            state[q] = yp[C:] + n_mat
        y_rows.append(jnp.concatenate(ys, axis=1))
    y = jnp.concatenate(y_rows, axis=0)

    inv_n = np.float32(1.0 / C_HEAD_DIM)
    mean = head_sum(y) * inv_n
    d = y - mean
    var = head_sum(d * d) * inv_n
    yn = d * lax.rsqrt(var + RWKV_GN_EPS) * gng_ref[...] + gnb_ref[...]
    out = (yn + bonus) * g
    y_ref[...] = out[:n_in].reshape(y_ref.shape)
    pout_ref[0] = state[...]


def _rwkv(pc, shift_in, p0, nb, seq, C, n_blk, arrs):
    gl = RWKV_GROUP * C_HEAD_DIM
    ngroup = C_WIDTH // gl
    if seq >= C:
        rows_blk = C * n_blk
        assert seq % rows_blk == 0, (seq, C, n_blk)
        n_valid, n_steps = C, seq // rows_blk
        pc_in, pc_spec = pc, pl.BlockSpec((rows_blk, C_COLS), lambda b, n: (b * n_steps + n, 0))
        y_shape = (pc.shape[0], C_WIDTH)
        y_spec = pl.BlockSpec((rows_blk, C_WIDTH), lambda b, n: (b * n_steps + n, 0))
    else:
        assert n_blk == 1
        rows_blk, n_valid, n_steps = C, seq, 1
        pc_in = pc.reshape(nb, seq, C_COLS)
        pc_spec = pl.BlockSpec((1, seq, C_COLS), lambda b, n: (b, 0, 0))
        y_shape = (nb, seq, C_WIDTH)
        y_spec = pl.BlockSpec((1, seq, C_WIDTH), lambda b, n: (b, 0, 0))
    y, pout = pl.pallas_call(
        functools.partial(_rwkv_kernel, chunk=C, n_blk=n_blk, n_valid=n_valid),
        grid=(nb, n_steps),
        in_specs=[pc_spec, pl.BlockSpec((1, 1, C_COLS), lambda b, n: (b, 0, 0)),
                  pl.BlockSpec((1, ngroup, gl, gl), lambda b, n: (b, 0, 0, 0))]
                 + [_const_spec(a.shape) for a in arrs],
        out_specs=[y_spec, pl.BlockSpec((1, ngroup, gl, gl), lambda b, n: (b, 0, 0, 0))],
        out_shape=[jax.ShapeDtypeStruct(y_shape, F32), jax.ShapeDtypeStruct((nb, ngroup, gl, gl), F32)],
        scratch_shapes=[pltpu.VMEM((ngroup, gl, gl), F32), pltpu.VMEM((rows_blk + 8, C_COLS), F32)],
        compiler_params=_cparams(("arbitrary", "arbitrary")),
        name="rwkv7",
    )(pc_in, shift_in, p0, *arrs)
    return y.reshape(-1, C_WIDTH), pout


def _state_to_groups(s):
    nb, hg, n = s.shape[0], RWKV_GROUP, C_HEAD_DIM
    p = jnp.swapaxes(s, -1, -2).reshape(nb, C_HEADS // hg, hg, n, n)
    eye = jnp.eye(hg, dtype=s.dtype)
    return (eye[None, None, :, None, :, None] * p[:, :, :, :, None, :]).reshape(nb, C_HEADS // hg, hg * n, hg * n)


def _groups_to_state(p):
    nb, hg, n = p.shape[0], RWKV_GROUP, C_HEAD_DIM
    p6 = p.reshape(nb, C_HEADS // hg, hg, n, hg, n)
    s = jnp.stack([p6[:, :, h, :, h, :] for h in range(hg)], axis=2).reshape(nb, C_HEADS, n, n)
    return jnp.swapaxes(s, -1, -2)


def _softmax_step(s, kc, m_ref, acc_ref, idx):
    m_old = m_ref[idx]
    m_new = jnp.maximum(m_old, jnp.max(s, axis=-1, keepdims=True))
    p = jnp.exp2((s - m_new) * EXP2_SCALE)
    corr = jnp.exp2((m_old - m_new) * EXP2_SCALE)
    acc_ref[idx] = acc_ref[idx] * corr + jnp.dot(p.astype(BF16), kc, preferred_element_type=F32)
    m_ref[idx] = m_new


def _softmax_result(acc):
    return acc[:, :KV_LORA] / acc[:, KV_LORA + ONES_LANE:KV_LORA + ONES_LANE + 1]


def _attn_prompt_kernel(qi_ref, kj_ref, q_ref, k_ref, o_ref, m_s, acc, *, tq, tk):
    s_id = pl.program_id(1)
    qi, kj = qi_ref[s_id], kj_ref[s_id]
    last = (qi * tq + tq - 1) // tk

    @pl.when(kj == 0)
    def _():
        m_s[...] = jnp.full(m_s.shape, NEG_INF, F32)
        acc[...] = jnp.zeros(acc.shape, F32)

    def update(masked):
        kc = k_ref[...]
        if masked:
            qpos = qi * tq + lax.broadcasted_iota(jnp.int32, (tq, tk), 0)
            kpos = kj * tk + lax.broadcasted_iota(jnp.int32, (tq, tk), 1)
            visible = qpos >= kpos
        for h in range(D_HEADS):
            s = lax.dot_general(q_ref[:, h * 2 * LANES:(h + 1) * 2 * LANES], kc, (((1,), (1,)), ((), ())),
                                preferred_element_type=F32)
            if masked:
                s = jnp.where(visible, s, NEG_INF)
            _softmax_step(s, kc, m_s, acc, h)

    @pl.when(kj < last)
    def _():
        update(False)

    @pl.when(kj == last)
    def _():
        update(True)
        o_ref[...] = jnp.concatenate([_softmax_result(acc[h]) for h in range(D_HEADS)], axis=1).astype(BF16)


def _attn_prompt(qcat, kcat, nb, seq, tq, tk):
    nq, nk = seq // tq, seq // tk
    pairs = [(qi, kj) for qi in range(nq) for kj in range((qi * tq + tq - 1) // tk + 1)]
    qi_tbl = jnp.asarray([p[0] for p in pairs], jnp.int32)
    kj_tbl = jnp.asarray([p[1] for p in pairs], jnp.int32)
    grid_spec = pltpu.PrefetchScalarGridSpec(
        num_scalar_prefetch=2,
        grid=(nb, len(pairs)),
        in_specs=[pl.BlockSpec((tq, D_HEADS * 2 * LANES), lambda b, s, qt, kt: (b * nq + qt[s], 0)),
                  pl.BlockSpec((tk, 2 * LANES), lambda b, s, qt, kt: (b * nk + kt[s], 0))],
        out_specs=pl.BlockSpec((tq, D_HEADS * KV_LORA), lambda b, s, qt, kt: (b * nq + qt[s], 0)),
        scratch_shapes=[pltpu.VMEM((D_HEADS, tq, 1), F32), pltpu.VMEM((D_HEADS, tq, 2 * LANES), F32)])
    return pl.pallas_call(
        functools.partial(_attn_prompt_kernel, tq=tq, tk=tk),
        grid_spec=grid_spec,
        out_shape=jax.ShapeDtypeStruct((nb * seq, D_HEADS * KV_LORA), BF16),
        compiler_params=_cparams(("arbitrary", "arbitrary")),
        name="attn_prompt",
    )(qi_tbl, kj_tbl, qcat, kcat)


def _attn_paged_kernel(pt_ref, q_ref, knew_ref, ckv_hbm, kpe_hbm, o_ref, ckv_buf, kpe_buf, sems, q_s, m_s, l_s,
                       acc, *, ls, gp):
    b, j = pl.program_id(0), pl.program_id(1)
    nbatch, ngroups = pl.num_programs(0), pl.num_programs(1)
    step = b * ngroups + j
    slot = step % 2

    def page_copies(bb, jj, sl):
        out = []
        for g in range(gp):
            page = pt_ref[bb, jj * gp + g]
            toks = pl.ds(g * PAGE_SIZE, PAGE_SIZE)
            out.append(pltpu.make_async_copy(ckv_hbm.at[page], ckv_buf.at[sl, toks], sems.at[sl]))
            out.append(pltpu.make_async_copy(kpe_hbm.at[page], kpe_buf.at[sl, :, toks], sems.at[sl]))
        return out

    @pl.when(step == 0)
    def _():
        for c in page_copies(b, j, slot):
            c.start()

    @pl.when(step + 1 < nbatch * ngroups)
    def _():
        nxt = step + 1
        for c in page_copies(nxt // ngroups, nxt % ngroups, 1 - slot):
            c.start()

    rows = D_HEADS * ls

    @pl.when(j == 0)
    def _():
        qc = q_ref[0]
        q_s[...] = jnp.concatenate(
            [qc[:, h * 2 * LANES:(h + 1) * 2 * LANES] for h in range(D_HEADS)], axis=0).astype(BF16)
        m_s[...] = jnp.full(m_s.shape, NEG_INF, F32)
        l_s[...] = jnp.zeros(l_s.shape, F32)
        acc[...] = jnp.zeros(acc.shape, F32)

    for c in page_copies(b, j, slot):
        c.wait()

    nt = (((1,), (1,)), ((), ()))
    q_lat = q_s[:, :KV_LORA]
    q_pe = q_s[:, KV_LORA:KV_LORA + D_ROPE]

    def online(s, vals):
        m_old = m_s[...]
        m_new = jnp.maximum(m_old, jnp.max(s, axis=-1, keepdims=True))
        p = jnp.exp2((s - m_new) * EXP2_SCALE)
        corr = jnp.exp2((m_old - m_new) * EXP2_SCALE)
        l_s[...] = l_s[...] * corr + jnp.sum(p, axis=-1, keepdims=True)
        acc[...] = acc[...] * corr + jnp.dot(p.astype(BF16), vals, preferred_element_type=F32)
        m_s[...] = m_new

    ck = ckv_buf[slot].astype(BF16)
    kp_t = kpe_buf[slot].astype(BF16)
    online(lax.dot_general(q_lat, ck, nt, preferred_element_type=F32)
           + jnp.dot(q_pe, kp_t, preferred_element_type=F32), ck)

    @pl.when(j == ngroups - 1)
    def _():
        kn = knew_ref[0]
        pad = jnp.concatenate([kn, jnp.zeros((LANES - ls, 2 * LANES), F32)], axis=0).astype(BF16)
        sn = lax.dot_general(q_s[...], pad, nt, preferred_element_type=F32)
        tq_pos = lax.broadcasted_iota(jnp.int32, (rows, LANES), 0) % ls
        tk_pos = lax.broadcasted_iota(jnp.int32, (rows, LANES), 1)
        online(jnp.where(tq_pos >= tk_pos, sn, NEG_INF), pad[:, :KV_LORA])
        o = acc[...] / l_s[...]
        o_ref[0] = jnp.concatenate([o[h * ls:(h + 1) * ls] for h in range(D_HEADS)], axis=1)


def _attn_paged(page_table, qcat3, knew3, cache_ckv, cache_kpe_t, gp):
    nbatch, ls = qcat3.shape[0], qcat3.shape[1]
    n_pages = page_table.shape[1]
    rows = D_HEADS * ls
    grid_spec = pltpu.PrefetchScalarGridSpec(
        num_scalar_prefetch=1,
        grid=(nbatch, n_pages // gp),
        in_specs=[pl.BlockSpec((1, ls, D_HEADS * 2 * LANES), lambda b, j, pt: (b, 0, 0)),
                  pl.BlockSpec((1, ls, 2 * LANES), lambda b, j, pt: (b, 0, 0)),
                  pl.BlockSpec(memory_space=pl.ANY), pl.BlockSpec(memory_space=pl.ANY)],
        out_specs=pl.BlockSpec((1, ls, D_HEADS * KV_LORA), lambda b, j, pt: (b, 0, 0)),
        scratch_shapes=[pltpu.VMEM((2, gp * PAGE_SIZE, KV_LORA), F32), pltpu.VMEM((2, D_ROPE, gp * PAGE_SIZE), F32),
                        pltpu.SemaphoreType.DMA((2,)), pltpu.VMEM((rows, 2 * LANES), BF16),
                        pltpu.VMEM((rows, 1), F32), pltpu.VMEM((rows, 1), F32), pltpu.VMEM((rows, KV_LORA), F32)])
    return pl.pallas_call(
        functools.partial(_attn_paged_kernel, ls=ls, gp=gp),
        grid_spec=grid_spec,
        out_shape=jax.ShapeDtypeStruct((nbatch, ls, D_HEADS * KV_LORA), F32),
        compiler_params=_cparams(("arbitrary", "arbitrary")),
        name="attn_paged",
    )(page_table, qcat3, knew3, cache_ckv, cache_kpe_t)


def _odd_out_kernel(x_ref, yc_ref, ol_ref, wuv_ref, wo_ref, lg_ref, lb_ref, o_ref):
    yd = _dot(ol_ref[...], wuv_ref[...])
    y = _dot(jnp.concatenate([yc_ref[...], yd], axis=1), wo_ref[...])
    o_ref[...] = _layernorm(ALPHA * x_ref[...] + y, lg_ref[...], lb_ref[...])


def _odd_out(x2d, yc, olat, tm, arrs):
    rows = x2d.shape[0]
    row_spec = lambda w: pl.BlockSpec((tm, w), lambda i: (i, 0))
    return pl.pallas_call(
        _odd_out_kernel,
        grid=(rows // tm,),
        in_specs=[row_spec(D_MODEL), row_spec(C_WIDTH), row_spec(D_HEADS * KV_LORA)]
                 + [_const_spec(a.shape) for a in arrs],
        out_specs=row_spec(D_MODEL),
        out_shape=jax.ShapeDtypeStruct((rows, D_MODEL), F32),
        compiler_params=_cparams(("arbitrary",)),
        name="odd_out",
    )(x2d, yc, olat, *arrs)


def _block_diag(blocks):
    n, r, c = blocks.shape
    eye = jnp.eye(n, dtype=blocks.dtype)
    return (eye[:, None, :, None] * blocks[:, :, None, :]).reshape(n * r, n * c)


def _rope_tables(positions):
    half = D_ROPE // 2
    inv = ROPE_BASE ** (-jnp.arange(half, dtype=F32) / half)
    ang = positions[:, None] * inv
    cos, sin = jnp.cos(ang), jnp.sin(ang)
    reps = LANES // D_ROPE
    return jnp.tile(jnp.concatenate([cos, cos], axis=1), (1, reps)), jnp.tile(jnp.concatenate([-sin, sin], axis=1), (1, reps))


def _tile_rows(n, cap):
    t = min(n, cap)
    assert n % t == 0, (n, cap)
    return t


def kernel(x_prompt, x_sample, state_pool_buf, state_rwkv, state_shift, cache_ckv, cache_kpe, page_table, ln_g, ln_b, w_in_even, pool_w, pool_scale, sgu_ln_g, sgu_ln_b, sgu_w, sgu_b, w_out_even, w_in_odd, rwkv_mu, rwkv_w0, rwkv_w2, rwkv_a0, rwkv_a2, rwkv_g2, rwkv_kk, rwkv_ka, rwkv_rk, rwkv_gn_g, rwkv_gn_b, mla_gq, mla_gkv, mla_wuq, mla_wuk, mla_wuv, w_out_odd, w_router, router_bias, w_gate, w_up, w_down):
    bp, lp, _ = x_prompt.shape
    bs, ls, _ = x_sample.shape
    n_pages = page_table.shape[1]
    past_len = n_pages * PAGE_SIZE
    assert ls == 8 and lp % SGU_CHUNK == 0 and lp % RWKV_CHUNK == 0
    rows_p, rows_s = bp * lp, bs * ls
    tm_p = _tile_rows(lp, 512)
    tm_s = _tile_rows(rows_s, 512)
    tm_moe_p = _tile_rows(rows_p, 512)
    tm_moe_s = _tile_rows(rows_s, 512)

    xp = x_prompt.reshape(rows_p, D_MODEL)
    xs = x_sample.reshape(rows_s, D_MODEL)

    wr_pad = jnp.pad(w_router, ((0, 0), (0, LANES - N_EXPERTS)))
    rb_pad = jnp.pad(router_bias, (0, LANES - N_EXPERTS)).reshape(1, LANES)
    wg, wu, wd = w_gate.astype(BF16), w_up.astype(BF16), w_down.astype(BF16)

    outs = {}
    for layer in range(DEPTH):
        i = layer // 2
        g1, b1 = ln_g[layer, 0], ln_b[layer, 0]
        if layer % 2 == 0:
            wi_split, wo_split = _split_weight(w_in_even[i]), _split_weight(w_out_even[i])
            wts_p = _even_weights(wi_split, pool_w[i], pool_scale[i], sgu_ln_g[i], sgu_ln_b[i], sgu_w[i],
                                  sgu_b[i].T, wo_split, g1, b1)
            reps = SGU_CHUNK // ls
            w_small = jnp.stack([_block_diag(jnp.broadcast_to(sgu_w[i][h, :ls, :ls], (reps, ls, ls)))
                                 for h in range(B_HEADS)])
            b_small = jnp.tile(sgu_b[i][:, :ls], (1, reps)).T
            wts_s = _even_weights(wi_split, pool_w[i], pool_scale[i], sgu_ln_g[i], sgu_ln_b[i], w_small,
                                  b_small, wo_split, g1, b1)
            xp, tail = _even_prompt(xp, bp, lp, _tile_rows(lp, 256), wts_p)
            bufpad = jnp.pad(state_pool_buf[i], ((0, 0), (HALO - POOL_BUF, 0), (0, 0)))
            xs, a_new, vn_s = _even_sample(xs, bufpad, ls, tm_s, wts_s)
            outs.setdefault("pool_p", []).append(tail.reshape(bp, HALO, A_WIDTH)[:, HALO - POOL_BUF:])
            ext = jnp.concatenate([state_pool_buf[i], a_new.reshape(bs, ls, A_WIDTH)], axis=1)
            outs.setdefault("pool_s", []).append(ext[:, -POOL_BUF:])
            outs.setdefault("sgu_s", []).append(vn_s.reshape(bs, ls, B_WIDTH))
        else:
            nope = D_HEADS * D_NOPE
            wq_cols = mla_wuq[i].reshape(Q_LORA, D_HEADS, D_NOPE + D_ROPE)
            wuq_perm = jnp.concatenate([wq_cols[:, :, :D_NOPE].reshape(Q_LORA, nope),
                                        wq_cols[:, :, D_NOPE:].reshape(Q_LORA, D_HEADS * D_ROPE)], axis=1)
            wuk_bd = _block_diag(jnp.swapaxes(mla_wuk[i], 1, 2))
            wuv_bd = _block_diag(mla_wuv[i])
            w_in = w_in_odd[i]
            w_in_pad = jnp.concatenate([w_in[:, :C_COLS + Q_LORA + KV_LORA],
                                        jnp.tile(w_in[:, C_COLS + Q_LORA + KV_LORA:], (1, LANES // D_ROPE))], axis=1)
            in_arrs = [w_in_pad.astype(BF16), mla_gq[i].reshape(1, -1), mla_gkv[i].reshape(1, -1),
                       wuq_perm.astype(BF16), wuk_bd.astype(BF16)]
            cos_p, sin_p = _rope_tables(jnp.arange(lp, dtype=F32))
            cos_s, sin_s = _rope_tables(past_len + jnp.arange(ls, dtype=F32))
            cos_s, sin_s = jnp.tile(cos_s, (tm_s // ls, 1)), jnp.tile(sin_s, (tm_s // ls, 1))
            pc_p, qcat_p, ckv_p, kpe_p, kcat_p = _odd_in(xp, cos_p, sin_p, tm_p, lp // tm_p, in_arrs, BF16)
            pc_s, qcat_s, ckv_s, kpe_s, kcat_s = _odd_in(xs, cos_s, sin_s, tm_s, 1, in_arrs, F32)

            z64 = jnp.zeros((C_DECAY_LORA, C_WIDTH), F32)
            wa2 = jnp.concatenate([jnp.concatenate([rwkv_w2[i], z64], axis=1),
                                   jnp.concatenate([z64, rwkv_a2[i]], axis=1)], axis=0)
            row = lambda v: v.reshape(1, -1)
            rw_arrs = [row(rwkv_mu[i]), row(rwkv_w0[i]), wa2.astype(BF16), row(rwkv_a0[i]), rwkv_g2[i].astype(BF16),
                       row(rwkv_kk[i]), row(rwkv_ka[i]), row(rwkv_rk[i]), row(rwkv_gn_g[i]), row(rwkv_gn_b[i]),
                       _block_diag(jnp.ones((C_HEADS, C_HEAD_DIM, C_HEAD_DIM), BF16))]
            zero_state = jnp.zeros((bp, C_HEADS, C_HEAD_DIM, C_HEAD_DIM), F32)
            yc_p, st_p = _rwkv(pc_p, jnp.zeros((bp, 1, C_COLS), F32), _state_to_groups(zero_state),
                               bp, lp, RWKV_CHUNK, RWKV_BLOCK, rw_arrs)
            yc_s, st_s = _rwkv(pc_s, state_shift[i], _state_to_groups(state_rwkv[i]), bs, ls, RWKV_CHUNK_SHORT, 1,
                               rw_arrs)

            ol_p = _attn_prompt(qcat_p, kcat_p, bp, lp, _tile_rows(lp, 256), _tile_rows(lp, 512))
            ol_s = _attn_paged(page_table, qcat_s.reshape(bs, ls, -1), kcat_s.reshape(bs, ls, -1),
                               cache_ckv[i], jnp.swapaxes(cache_kpe[i], 1, 2), math.gcd(n_pages, PAGES_PER_STEP))

            out_arrs = [wuv_bd.astype(BF16), w_out_odd[i].astype(BF16), g1.reshape(1, -1), b1.reshape(1, -1)]
            xp_new = _odd_out(xp, yc_p, ol_p, tm_p, out_arrs)
            xs_new = _odd_out(xs, yc_s, ol_s.reshape(rows_s, -1), tm_s, out_arrs)

            outs.setdefault("rwkv_p", []).append(_groups_to_state(st_p))
            outs.setdefault("rwkv_s", []).append(_groups_to_state(st_s))
            outs.setdefault("shift_p", []).append(pc_p.reshape(bp, lp, C_COLS)[:, -1:])
            outs.setdefault("shift_s", []).append(pc_s.reshape(bs, ls, C_COLS)[:, -1:])
            outs.setdefault("ckv_p", []).append(ckv_p.reshape(bp, lp, KV_LORA))
            outs.setdefault("ckv_s", []).append(ckv_s.reshape(bs, ls, KV_LORA))
            outs.setdefault("kpe_p", []).append(kpe_p.reshape(bp, lp, D_ROPE))
            outs.setdefault("kpe_s", []).append(kpe_s.reshape(bs, ls, D_ROPE))
            xp, xs = xp_new, xs_new
        g2, b2 = ln_g[layer, 1], ln_b[layer, 1]
        xp = _moe(xp, tm_moe_p, wr_pad, rb_pad, wg[layer], wu[layer], wd[layer], g2, b2)
        xs = _moe(xs, tm_moe_s, wr_pad, rb_pad, wg[layer], wu[layer], wd[layer], g2, b2)

    st = lambda k: jnp.stack(outs[k])
    return (xp.reshape(bp, lp, D_MODEL), xs.reshape(bs, ls, D_MODEL), st("pool_p"), st("pool_s"), st("sgu_s"),
            st("rwkv_p"), st("rwkv_s"), st("shift_p"), st("shift_s"), st("ckv_p"), st("ckv_s"), st("kpe_p"),
            st("kpe_s"))
```

```python
import functools
import math

import jax
import jax.numpy as jnp
import numpy as np
from jax import lax
from jax.experimental import pallas as pl
from jax.experimental.pallas import tpu as pltpu

F32 = jnp.float32
BF16 = jnp.bfloat16

D_MODEL = 1024
DEPTH = 2
ALPHA = (2 * DEPTH) ** 0.25
LN_EPS = 1e-5
RMS_EPS = 1e-6
A_WIDTH = 512
POOL_WINDOWS = (2, 4, 8, 16)
HALO = 16
POOL_BUF = 15
B_WIDTH = 512
B_HEADS = 4
SGU_CHUNK = 128
C_WIDTH = 512
C_HEAD_DIM = 64
C_HEADS = 8
C_COLS = 1792
RWKV_GN_EPS = 64e-5
C_DECAY_LORA = 64
RWKV_CHUNK = 64
RWKV_BLOCK = 4
RWKV_CHUNK_SHORT = 16
RWKV_GROUP = 4
D_HEADS = 8
D_NOPE = 64
D_ROPE = 32
Q_LORA = 256
KV_LORA = 128
MLA_SCALE = (D_NOPE + D_ROPE) ** -0.5
ROPE_BASE = 10000.0
ONES_LANE = D_ROPE
EXP2_SCALE = MLA_SCALE * math.log2(math.e)
PAGE_SIZE = 128
PAGES_PER_STEP = 64
DECODE_SPLIT = 1
N_EXPERTS = 16
D_EXPERT = 256
LANES = 128
VMEM_LIMIT = 56 * 1024 * 1024

NEG_INF = float("-inf")


def _cparams(sem):
    return pltpu.CompilerParams(dimension_semantics=sem, vmem_limit_bytes=VMEM_LIMIT)


def _dot(a, b):
    return jnp.dot(a.astype(BF16), b.astype(BF16), preferred_element_type=F32)


def _dot_nt(a, b):
    return lax.dot_general(a.astype(BF16), b.astype(BF16), (((1,), (1,)), ((), ())), preferred_element_type=F32)


def _dot_tn(a, b):
    return lax.dot_general(a.astype(BF16), b.astype(BF16), (((0,), (0,)), ((), ())), preferred_element_type=F32)


def _split3(a):
    a1 = a.astype(BF16)
    r1 = a - a1.astype(F32)
    a2 = r1.astype(BF16)
    a3 = (r1 - a2.astype(F32)).astype(BF16)
    return a1, a2, a3


def _split2(a):
    hi = a.astype(BF16)
    return hi, (a - hi.astype(F32)).astype(BF16)


def _dot3(a, b_hi, b_lo):
    a_hi, a_lo = _split2(a)
    return (jnp.dot(a_hi, b_hi, preferred_element_type=F32)
            + (jnp.dot(a_hi, b_lo, preferred_element_type=F32) + jnp.dot(a_lo, b_hi, preferred_element_type=F32)))


def _layernorm(r, g, b):
    mu = jnp.mean(r, axis=-1, keepdims=True)
    d = r - mu
    var = jnp.mean(d * d, axis=-1, keepdims=True)
    return d * lax.rsqrt(var + LN_EPS) * g + b


def _rmsnorm(x, g):
    return x * lax.rsqrt(jnp.mean(x * x, axis=-1, keepdims=True) + RMS_EPS) * g


def _gelu(x):
    return 0.5 * x * (1.0 + lax.erf(x * np.float32(math.sqrt(0.5))))


def _even_tail(x, z, pooled, wrefs, tm):
    pw_ref, ps_ref, sg_ref, sb_ref, sw_ref, sbc_ref, woh_ref, wol_ref, lg_ref, lb_ref = wrefs
    u, v = z[:, :B_WIDTH], z[:, B_WIDTH:]
    a_out = jnp.concatenate(
        [_dot3(pooled[g], *_split2(pw_ref[g])) for g in range(len(POOL_WINDOWS))], axis=1) * ps_ref[...]
    row = lax.broadcasted_iota(jnp.int32, (SGU_CHUNK, SGU_CHUNK), 0)
    col = lax.broadcasted_iota(jnp.int32, (SGU_CHUNK, SGU_CHUNK), 1)
    vn_parts, b_parts = [], []
    for h in range(B_HEADS):
        sl = slice(h * LANES, (h + 1) * LANES)
        vh = v[:, sl]
        mu = jnp.mean(vh, axis=-1, keepdims=True)
        d = vh - mu
        var = jnp.mean(d * d, axis=-1, keepdims=True)
        vn = d * lax.rsqrt(var + LN_EPS) * sg_ref[:, sl] + sb_ref[:, sl]
        vn_parts.append(vn)
        wm = jnp.where(col <= row, sw_ref[h], 0.0)
        bias = sbc_ref[:, h:h + 1]
        v_hi, v_lo = _split2(vn)
        mix = jnp.concatenate(
            [_dot3(wm, v_hi[n * SGU_CHUNK:(n + 1) * SGU_CHUNK], v_lo[n * SGU_CHUNK:(n + 1) * SGU_CHUNK]) + bias
             for n in range(tm // SGU_CHUNK)], axis=0)
        b_parts.append(u[:, sl] * mix)
    vn_all = jnp.concatenate(vn_parts, axis=1)
    cat = jnp.concatenate([a_out] + b_parts, axis=1)
    y = _dot3(cat, woh_ref[...], wol_ref[...])
    return _layernorm(ALPHA * x + y, lg_ref[...], lb_ref[...]), vn_all


def _even_prompt_kernel(x_ref, wih_ref, wil_ref, *rest, tm):
    wrefs, (xo_ref, tail_ref, abuf) = rest[:-3], rest[-3:]
    t = pl.program_id(1)
    x = x_ref[...]
    p = _dot3(x, wih_ref[...], wil_ref[...])
    a = p[:, :A_WIDTH]
    z = _gelu(p[:, A_WIDTH:])

    @pl.when(t == 0)
    def _():
        abuf[0:HALO, :] = jnp.zeros((HALO, A_WIDTH), F32)

    abuf[HALO:HALO + tm, :] = a
    pos = t * tm + lax.broadcasted_iota(jnp.int32, (tm, LANES), 0)
    pooled = []
    for g, w in enumerate(POOL_WINDOWS):
        sl = slice(g * LANES, (g + 1) * LANES)
        acc = a[:, sl]
        for s in range(1, w):
            acc = acc + abuf[HALO - s:HALO - s + tm, sl]
        cnt = jnp.minimum(w, pos + 1).astype(F32)
        pooled.append(acc / cnt - a[:, sl])
    tail = abuf[tm:tm + HALO, :]
    abuf[0:HALO, :] = tail
    tail_ref[...] = tail
    xo, _ = _even_tail(x, z, pooled, wrefs, tm)
    xo_ref[...] = xo


def _even_sample_kernel(x_ref, buf_ref, wih_ref, wil_ref, *rest, tm, ls):
    wrefs, (xo_ref, anew_ref, vn_ref, ext) = rest[:-4], rest[-4:]
    nseq = tm // ls
    x = x_ref[...]
    p = _dot3(x, wih_ref[...], wil_ref[...])
    a = p[:, :A_WIDTH]
    z = _gelu(p[:, A_WIDTH:])
    ext[:, 0:HALO, :] = buf_ref[...]
    ext[:, HALO:HALO + ls, :] = a.reshape(nseq, ls, A_WIDTH)
    pooled = []
    for g, w in enumerate(POOL_WINDOWS):
        sl = slice(g * LANES, (g + 1) * LANES)
        acc = ext[:, HALO:HALO + ls, sl]
        for s in range(1, w):
            acc = acc + ext[:, HALO - s:HALO - s + ls, sl]
        pooled.append(acc.reshape(tm, LANES) * np.float32(1.0 / w) - a[:, sl])
    xo, vn = _even_tail(x, z, pooled, wrefs, tm)
    xo_ref[...] = xo
    anew_ref[...] = a
    vn_ref[...] = vn


def _const_spec(shape):
    nd = len(shape)
    return pl.BlockSpec(shape, lambda *_: (0,) * nd)


def _split_weight_kernel(w_ref, hi_ref, lo_ref):
    hi, lo = _split2(w_ref[...])
    hi_ref[...] = hi
    lo_ref[...] = lo


def _split_weight(w):
    rows, cols = w.shape
    tr = _tile_rows(rows, 256)
    spec = pl.BlockSpec((tr, cols), lambda i: (i, 0))
    return pl.pallas_call(
        _split_weight_kernel,
        grid=(rows // tr,),
        in_specs=[spec],
        out_specs=[spec, spec],
        out_shape=[jax.ShapeDtypeStruct(w.shape, BF16)] * 2,
        compiler_params=_cparams(("arbitrary",)),
        name="split_weight",
    )(w)


def _even_weights(w_in_split, pool_w, pool_scale, sgu_g, sgu_b, sgu_w, sgu_bcol, w_out_split, ln_g, ln_b):
    arrs = [*w_in_split, pool_w, pool_scale.reshape(1, -1), sgu_g.reshape(1, -1),
            sgu_b.reshape(1, -1), sgu_w, sgu_bcol, *w_out_split, ln_g.reshape(1, -1), ln_b.reshape(1, -1)]
    return arrs, [_const_spec(a.shape) for a in arrs]


def _even_prompt(x2d, nb, seq, tm, wts):
    nt = seq // tm
    arrs, specs = wts
    row_spec = lambda w: pl.BlockSpec((tm, w), lambda b, t: (b * nt + t, 0))
    return pl.pallas_call(
        functools.partial(_even_prompt_kernel, tm=tm),
        grid=(nb, nt),
        in_specs=[row_spec(D_MODEL)] + specs,
        out_specs=[row_spec(D_MODEL), pl.BlockSpec((HALO, A_WIDTH), lambda b, t: (b, 0))],
        out_shape=[jax.ShapeDtypeStruct((nb * seq, D_MODEL), F32), jax.ShapeDtypeStruct((nb * HALO, A_WIDTH), F32)],
        scratch_shapes=[pltpu.VMEM((tm + HALO, A_WIDTH), F32)],
        compiler_params=_cparams(("arbitrary", "arbitrary")),
        name="even_prompt",
    )(x2d, *arrs)


def _even_sample(x2d, bufpad, ls, tm, wts):
    rows = x2d.shape[0]
    nseq = tm // ls
    arrs, specs = wts
    row_spec = lambda w: pl.BlockSpec((tm, w), lambda i: (i, 0))
    return pl.pallas_call(
        functools.partial(_even_sample_kernel, tm=tm, ls=ls),
        grid=(rows // tm,),
        in_specs=[row_spec(D_MODEL), pl.BlockSpec((nseq, HALO, A_WIDTH), lambda i: (i, 0, 0))] + specs,
        out_specs=[row_spec(D_MODEL), row_spec(A_WIDTH), row_spec(B_WIDTH)],
        out_shape=[jax.ShapeDtypeStruct((rows, D_MODEL), F32), jax.ShapeDtypeStruct((rows, A_WIDTH), F32),
                   jax.ShapeDtypeStruct((rows, B_WIDTH), F32)],
        scratch_shapes=[pltpu.VMEM((nseq, HALO + ls, A_WIDTH), F32)],
        compiler_params=_cparams(("arbitrary",)),
        name="even_sample",
    )(x2d, bufpad, *arrs)


def _route(scores, bias):
    shape = scores.shape
    lane = lax.broadcasted_iota(jnp.int32, shape, 1)
    pos_in_group = lane % 4
    group = lane // 4
    biased = scores + bias

    def from_lane(x, d):
        return pltpu.roll(x, d % LANES, 1)

    offsets = (-3, -2, -1, 1, 2, 3)
    rank = jnp.zeros(shape, jnp.int32)
    for d in offsets:
        src = pos_in_group - d
        valid = (src >= 0) & (src <= 3)
        other = from_lane(biased, d)
        beats = (other > biased) | ((other == biased) & (d > 0))
        rank = rank + jnp.where(valid & beats, 1, 0)
    top2 = rank < 2
    kept = jnp.where(top2, biased, 0.0)
    gsum = kept
    for d in offsets:
        src = pos_in_group - d
        valid = (src >= 0) & (src <= 3)
        gsum = gsum + jnp.where(valid, from_lane(kept, d), 0.0)
    lost = jnp.zeros(shape, jnp.int32)
    for dg in offsets:
        src = group - dg
        valid = (src >= 0) & (src <= 3)
        other = from_lane(gsum, 4 * dg)
        beats = (other > gsum) | ((other == gsum) & (dg > 0))
        lost = lost + jnp.where(valid & beats, 1, 0)
    selected = top2 & (lost == 0) & (lane < N_EXPERTS)
    picked = jnp.where(selected, scores, 0.0)
    return picked / jnp.sum(picked, axis=-1, keepdims=True)


def _moe_kernel(x_ref, wr_ref, rb_ref, wg_ref, wu_ref, wd_ref, lg_ref, lb_ref, o_ref, hbuf):
    x = x_ref[...]
    x1 = x.astype(BF16)
    x2 = (x - x1.astype(F32)).astype(BF16)
    wr = wr_ref[...]
    w1 = wr.astype(BF16)
    w2 = (wr - w1.astype(F32)).astype(BF16)
    hi_part = jnp.dot(x1, jnp.concatenate([w1, w2], axis=1), preferred_element_type=F32)
    logits = hi_part[:, :LANES] + (hi_part[:, LANES:] + jnp.dot(x2, w1, preferred_element_type=F32))
    gates = _route(jax.nn.sigmoid(logits), rb_ref[...])
    for e in range(N_EXPERTS):
        hg = jnp.dot(x1, wg_ref[e], preferred_element_type=F32)
        hu = jnp.dot(x1, wu_ref[e], preferred_element_type=F32)
        he = hg * jax.nn.sigmoid(hg) * hu * gates[:, e:e + 1]
        hbuf[:, e * D_EXPERT:(e + 1) * D_EXPERT] = he.astype(BF16)
    y = jnp.dot(hbuf[...], wd_ref[...], preferred_element_type=F32)
    o_ref[...] = _layernorm(ALPHA * x + y, lg_ref[...], lb_ref[...])


def _resident_spec(shape):
    nd = len(shape)
    return pl.BlockSpec(shape, lambda *_: (0,) * nd, pipeline_mode=pl.Buffered(1))


def _moe(x2d, tm, wr_pad, rb_pad, wg, wu, wd, ln_g, ln_b):
    rows = x2d.shape[0]
    row_spec = pl.BlockSpec((tm, D_MODEL), lambda i: (i, 0))
    wd_all = wd.reshape(N_EXPERTS * D_EXPERT, D_MODEL)
    return pl.pallas_call(
        _moe_kernel,
        grid=(rows // tm,),
        in_specs=[row_spec, _const_spec(wr_pad.shape), _const_spec(rb_pad.shape),
                  _resident_spec(wg.shape), _resident_spec(wu.shape), _resident_spec(wd_all.shape),
                  _const_spec((1, D_MODEL)), _const_spec((1, D_MODEL))],
        out_specs=row_spec,
        out_shape=jax.ShapeDtypeStruct((rows, D_MODEL), F32),
        scratch_shapes=[pltpu.VMEM((tm, N_EXPERTS * D_EXPERT), BF16)],
        compiler_params=_cparams(("arbitrary",)),
        name="moe",
    )(x2d, wr_pad, rb_pad, wg, wu, wd_all, ln_g.reshape(1, -1), ln_b.reshape(1, -1))


def _rope_lanes(x, cos_t, sin_t):
    lane = lax.broadcasted_iota(jnp.int32, x.shape, 1)
    partner = jnp.where((lane % D_ROPE) < D_ROPE // 2,
                        pltpu.roll(x, LANES - D_ROPE // 2, 1), pltpu.roll(x, D_ROPE // 2, 1))
    return x * cos_t + partner * sin_t


def _key_tail(kpe_lanes):
    lane = lax.broadcasted_iota(jnp.int32, kpe_lanes.shape, 1)
    return jnp.where(lane < D_ROPE, kpe_lanes, jnp.where(lane == ONES_LANE, 1.0, 0.0))


def _odd_in_kernel(x_ref, cos_ref, sin_ref, wi_ref, gq_ref, gkv_ref, wuq_ref, wuk_ref,
                   pc_ref, qcat_ref, ckv_ref, kpe_ref, kcat_ref, kcat_t_ref):
    p = _dot(x_ref[...], wi_ref[...])
    pc_ref[...] = p[:, :C_COLS]
    o = C_COLS
    cq = _rmsnorm(p[:, o:o + Q_LORA], gq_ref[...])
    ckv = _rmsnorm(p[:, o + Q_LORA:o + Q_LORA + KV_LORA], gkv_ref[...])
    cos_t, sin_t = cos_ref[...], sin_ref[...]
    kslot = _rope_lanes(p[:, o + Q_LORA + KV_LORA:o + Q_LORA + KV_LORA + LANES], cos_t, sin_t)
    q = _dot(cq, wuq_ref[...])
    nope = D_HEADS * D_NOPE
    qlat = _dot(q[:, :nope], wuk_ref[...])
    lane = lax.broadcasted_iota(jnp.int32, kslot.shape, 1)
    per_vreg = LANES // D_ROPE
    for i in range(D_HEADS // per_vreg):
        qpe = _rope_lanes(q[:, nope + i * LANES:nope + (i + 1) * LANES], cos_t, sin_t)
        for j in range(per_vreg):
            h = i * per_vreg + j
            pe = qpe if j == 0 else pltpu.roll(qpe, LANES - j * D_ROPE, 1)
            row = jnp.concatenate([qlat[:, h * KV_LORA:(h + 1) * KV_LORA], jnp.where(lane < D_ROPE, pe, 0.0)], axis=1)
            qcat_ref[:, h * 2 * LANES:(h + 1) * 2 * LANES] = row.astype(qcat_ref.dtype)
    ckv_ref[...] = ckv
    kpe_ref[...] = kslot[:, :D_ROPE]
    kcat = jnp.concatenate([ckv, _key_tail(kslot)], axis=1)
    kcat_ref[...] = kcat.astype(kcat_ref.dtype)
    kcat_t_ref[...] = kcat.T.astype(kcat_t_ref.dtype)


def _odd_in(x2d, cos_t, sin_t, tm, pos_blocks, arrs, qdt):
    rows = x2d.shape[0]
    row_spec = lambda w: pl.BlockSpec((tm, w), lambda i: (i, 0))
    pos_spec = pl.BlockSpec((tm, LANES), lambda i: (i % pos_blocks, 0))
    out_w = [(C_COLS, F32), (D_HEADS * 2 * LANES, qdt), (KV_LORA, F32), (D_ROPE, F32), (2 * LANES, qdt)]
    return pl.pallas_call(
        _odd_in_kernel,
        grid=(rows // tm,),
        in_specs=[row_spec(D_MODEL), pos_spec, pos_spec] + [_const_spec(a.shape) for a in arrs],
        out_specs=[row_spec(w) for w, _ in out_w] + [pl.BlockSpec((2 * LANES, tm), lambda i: (0, i))],
        out_shape=[jax.ShapeDtypeStruct((rows, w), dt) for w, dt in out_w]
                  + [jax.ShapeDtypeStruct((2 * LANES, rows), qdt)],
        compiler_params=_cparams(("arbitrary",)),
        name="odd_in",
    )(x2d, cos_t, sin_t, *arrs)


def _rwkv_kernel(pc_ref, shift_ref, p0_ref, mu_ref, w0_ref, wa2_ref, a0_ref, g2_ref, kkc_ref, kac_ref, rk_ref,
                 gng_ref, gnb_ref, ho_ref, y_ref, pout_ref, state, xbuf, *, chunk, n_blk, n_valid):
    C = chunk
    HG = RWKV_GROUP
    GR, GL = HG * C, HG * C_HEAD_DIM
    R = n_blk * C
    n_in = R if n_valid == C else n_valid
    n = pl.program_id(1)

    @pl.when(n == 0)
    def _():
        state[...] = p0_ref[0]
        xbuf[7:8, :] = shift_ref[0]

    pc = pc_ref[...].reshape(n_in, C_COLS)
    if n_in < R:
        pc = jnp.concatenate([pc, jnp.zeros((R - n_in, C_COLS), F32)], axis=0)
    xbuf[8:8 + R, :] = pc
    prev = xbuf[7:7 + R, :]
    xs = pc + (prev - pc) * mu_ref[...]
    xbuf[7:8, :] = pc[n_in - 1:n_in, :]

    W = C_WIDTH
    r, k, v = xs[:, :W], xs[:, W:2 * W], xs[:, 2 * W:3 * W]
    wa = xs[:, 3 * W:3 * W + LANES]
    gl = xs[:, 3 * W + LANES:]
    first_half_r = lax.broadcasted_iota(jnp.int32, (R, LANES), 1) < C_HEAD_DIM
    lora = _dot(jnp.where(first_half_r, jnp.tanh(wa), wa), wa2_ref[...])
    zw = -(w0_ref[...] + lora[:, :W])
    softplus = jnp.maximum(zw, 0.0) + jnp.log1p(jnp.exp(-jnp.abs(zw)))
    ld = -jnp.exp(-softplus - 0.5)
    a = jax.nn.sigmoid(a0_ref[...] + lora[:, W:])
    g = _dot(jax.nn.sigmoid(gl), g2_ref[...])

    head_ones = ho_ref[...]

    def head_sum(xv):
        return _dot(xv, head_ones)

    kk = k * kkc_ref[...]
    kp = k * (1.0 + (a - 1.0) * kac_ref[...])
    sums = head_sum(jnp.concatenate([kk * kk, r * kp * rk_ref[...]], axis=0))
    kk = kk * lax.rsqrt(sums[:R] + 1e-12)
    bonus = sums[R:] * v
    if n_in < R:
        valid = lax.broadcasted_iota(jnp.int32, (R, W), 0) < n_in
        ld = jnp.where(valid, ld, 0.0)
        kk = jnp.where(valid, kk, 0.0)
        kp = jnp.where(valid, kp, 0.0)
        v = jnp.where(valid, v, 0.0)
    b = kk * a

    tr = lax.broadcasted_iota(jnp.int32, (R, R), 0)
    tc = lax.broadcasted_iota(jnp.int32, (R, R), 1)
    tri = jnp.where(((tr // C) == (tc // C)) & (tc <= tr), 1.0, 0.0).astype(BF16)
    lam3 = jnp.dot(tri, jnp.concatenate(_split3(ld), axis=1), preferred_element_type=F32)
    lam = lam3[:, :W] + lam3[:, W:2 * W] + lam3[:, 2 * W:]
    lam_ends = [lam[c * C + C - 1:c * C + C, :] for c in range(n_blk)]
    lam_end = jnp.concatenate([jnp.broadcast_to(le, (C, W)) for le in lam_ends], axis=0)
    e_neg = jnp.exp(-lam)
    e_end = jnp.exp(lam_end - lam)
    rt = r * jnp.exp(lam)
    kt = kp * e_neg
    kkt = kk * jnp.exp(lam - ld)
    bt = b * e_neg
    kg = kp * e_end
    bg = b * e_end

    r2 = lax.broadcasted_iota(jnp.int32, (GR, GR), 0)
    c2 = lax.broadcasted_iota(jnp.int32, (GR, GR), 1)
    same_blk = (r2 // C) == (c2 // C)
    strict = same_blk & ((c2 % C) < (r2 % C))
    incl = same_blk & ((c2 % C) <= (r2 % C))
    eye_r = jnp.where(r2 == c2, 1.0, 0.0)
    eye_l = (lax.broadcasted_iota(jnp.int32, (GL, GL), 0) == lax.broadcasted_iota(jnp.int32, (GL, GL), 1))
    head_of_lane = lax.broadcasted_iota(jnp.int32, (C, GL), 1) // C_HEAD_DIM

    def stack(xv, c, q):
        xq = xv[c * C:(c + 1) * C, q * GL:(q + 1) * GL]
        return jnp.concatenate([jnp.where(head_of_lane == h, xq, 0.0) for h in range(HG)], axis=0)

    def unstack(xz):
        out = xz[:C]
        for h in range(1, HG):
            out = out + xz[h * C:(h + 1) * C]
        return out

    ngroup = W // GL
    pre = []
    for c, q in [(c, q) for c in range(n_blk) for q in range(ngroup)]:
        gz, rz, bz, kz, vz = stack(kkt, c, q), stack(rt, c, q), stack(bt, c, q), stack(kt, c, q), stack(v, c, q)
        bgz, kgz = stack(bg, c, q), stack(kg, c, q)
        sc = _dot_nt(jnp.concatenate([gz, rz], axis=0), jnp.concatenate([bz, kz], axis=0))
        aab = jnp.where(strict, sc[:GR, :GR], 0.0)
        aak = jnp.where(strict, sc[:GR, GR:], 0.0)
        rrb = jnp.where(incl, sc[GR:, :GR], 0.0)
        rrk = jnp.where(incl, sc[GR:, GR:], 0.0)
        tinv = eye_r - aab
        xp = _dot(aab, aab)
        for _ in range(int(math.log2(C)) - 2):
            sq = _dot(xp, jnp.concatenate([tinv, xp], axis=1))
            tinv = tinv + sq[:, :GR]
            xp = sq[:, GR:]
        tinv = tinv + _dot(xp, tinv)
        akv = _dot(aak, vz)
        wu = _dot(tinv, jnp.concatenate([gz, akv], axis=1))
        bwu = _dot_tn(bgz, wu)
        g_end_q = jnp.exp(lam_ends[c][:, q * GL:(q + 1) * GL])
        m_mat = jnp.where(eye_l, g_end_q, 0.0) - bwu[:, :GL]
        n_mat = _dot_tn(kgz, vz) - bwu[:, GL:]
        rwu = _dot(rrb, wu)
        y0 = unstack(_dot(rrk, vz) - rwu[:, GL:])
        pre.append((jnp.concatenate([unstack(rz - rwu[:, :GL]), m_mat], axis=0), y0, n_mat))

    y_rows = []
    for c in range(n_blk):
        ys = []
        for q in range(ngroup):
            qm_m, y0, n_mat = pre[c * ngroup + q]
            yp = _dot(qm_m, state[q])
            ys.append(yp[:C] + y0)
            state[q] = yp[C:] + n_mat
        y_rows.append(jnp.concatenate(ys, axis=1))
    y = jnp.concatenate(y_rows, axis=0)

    inv_n = np.float32(1.0 / C_HEAD_DIM)
    mean = head_sum(y) * inv_n
    d = y - mean
    var = head_sum(d * d) * inv_n
    yn = d * lax.rsqrt(var + RWKV_GN_EPS) * gng_ref[...] + gnb_ref[...]
    out = (yn + bonus) * g
    y_ref[...] = out[:n_in].reshape(y_ref.shape)
    pout_ref[0] = state[...]


def _rwkv(pc, shift_in, p0, nb, seq, C, n_blk, arrs):
    gl = RWKV_GROUP * C_HEAD_DIM
    ngroup = C_WIDTH // gl
    if seq >= C:
        rows_blk = C * n_blk
        assert seq % rows_blk == 0, (seq, C, n_blk)
        n_valid, n_steps = C, seq // rows_blk
        pc_in, pc_spec = pc, pl.BlockSpec((rows_blk, C_COLS), lambda b, n: (b * n_steps + n, 0))
        y_shape = (pc.shape[0], C_WIDTH)
        y_spec = pl.BlockSpec((rows_blk, C_WIDTH), lambda b, n: (b * n_steps + n, 0))
    else:
        assert n_blk == 1
        rows_blk, n_valid, n_steps = C, seq, 1
        pc_in = pc.reshape(nb, seq, C_COLS)
        pc_spec = pl.BlockSpec((1, seq, C_COLS), lambda b, n: (b, 0, 0))
        y_shape = (nb, seq, C_WIDTH)
        y_spec = pl.BlockSpec((1, seq, C_WIDTH), lambda b, n: (b, 0, 0))
    y, pout = pl.pallas_call(
        functools.partial(_rwkv_kernel, chunk=C, n_blk=n_blk, n_valid=n_valid),
        grid=(nb, n_steps),
        in_specs=[pc_spec, pl.BlockSpec((1, 1, C_COLS), lambda b, n: (b, 0, 0)),
                  pl.BlockSpec((1, ngroup, gl, gl), lambda b, n: (b, 0, 0, 0))]
                 + [_const_spec(a.shape) for a in arrs],
        out_specs=[y_spec, pl.BlockSpec((1, ngroup, gl, gl), lambda b, n: (b, 0, 0, 0))],
        out_shape=[jax.ShapeDtypeStruct(y_shape, F32), jax.ShapeDtypeStruct((nb, ngroup, gl, gl), F32)],
        scratch_shapes=[pltpu.VMEM((ngroup, gl, gl), F32), pltpu.VMEM((rows_blk + 8, C_COLS), F32)],
        compiler_params=_cparams(("arbitrary", "arbitrary")),
        name="rwkv7",
    )(pc_in, shift_in, p0, *arrs)
    return y.reshape(-1, C_WIDTH), pout


def _state_to_groups(s):
    nb, hg, n = s.shape[0], RWKV_GROUP, C_HEAD_DIM
    p = jnp.swapaxes(s, -1, -2).reshape(nb, C_HEADS // hg, hg, n, n)
    eye = jnp.eye(hg, dtype=s.dtype)
    return (eye[None, None, :, None, :, None] * p[:, :, :, :, None, :]).reshape(nb, C_HEADS // hg, hg * n, hg * n)


def _groups_to_state(p):
    nb, hg, n = p.shape[0], RWKV_GROUP, C_HEAD_DIM
    p6 = p.reshape(nb, C_HEADS // hg, hg, n, hg, n)
    s = jnp.stack([p6[:, :, h, :, h, :] for h in range(hg)], axis=2).reshape(nb, C_HEADS, n, n)
    return jnp.swapaxes(s, -1, -2)


def _softmax_step(s, kc, m_ref, acc_ref, idx):
    m_old = m_ref[idx]
    m_new = jnp.maximum(m_old, jnp.max(s, axis=-1, keepdims=True))
    p = jnp.exp2((s - m_new) * EXP2_SCALE)
    corr = jnp.exp2((m_old - m_new) * EXP2_SCALE)
    acc_ref[idx] = acc_ref[idx] * corr + jnp.dot(p.astype(BF16), kc, preferred_element_type=F32)
    m_ref[idx] = m_new


def _softmax_result(acc):
    return acc[:, :KV_LORA] / acc[:, KV_LORA + ONES_LANE:KV_LORA + ONES_LANE + 1]


def _attn_prompt_kernel(qi_ref, kj_ref, q_ref, k_ref, kt_ref, o_ref, m_s, acc, *, tq, tk):
    s_id = pl.program_id(1)
    qi, kj = qi_ref[s_id], kj_ref[s_id]
    last = (qi * tq + tq - 1) // tk

    @pl.when(kj == 0)
    def _():
        m_s[...] = jnp.full(m_s.shape, NEG_INF, F32)
        acc[...] = jnp.zeros(acc.shape, F32)

    def update(masked):
        kc = k_ref[...]
        kc_t = kt_ref[...]
        if masked:
            qpos = qi * tq + lax.broadcasted_iota(jnp.int32, (tq, tk), 0)
            kpos = kj * tk + lax.broadcasted_iota(jnp.int32, (tq, tk), 1)
            visible = qpos >= kpos
        for h in range(D_HEADS):
            s = jnp.dot(q_ref[:, h * 2 * LANES:(h + 1) * 2 * LANES], kc_t, preferred_element_type=F32)
            if masked:
                s = jnp.where(visible, s, NEG_INF)
            _softmax_step(s, kc, m_s, acc, h)

    @pl.when(kj < last)
    def _():
        update(False)

    @pl.when(kj == last)
    def _():
        update(True)
        o_ref[...] = jnp.concatenate([_softmax_result(acc[h]) for h in range(D_HEADS)], axis=1).astype(BF16)


def _attn_prompt(qcat, kcat, kcat_t, nb, seq, tq, tk):
    nq, nk = seq // tq, seq // tk
    pairs = [(qi, kj) for qi in range(nq) for kj in range((qi * tq + tq - 1) // tk + 1)]
    qi_tbl = jnp.asarray([p[0] for p in pairs], jnp.int32)
    kj_tbl = jnp.asarray([p[1] for p in pairs], jnp.int32)
    grid_spec = pltpu.PrefetchScalarGridSpec(
        num_scalar_prefetch=2,
        grid=(nb, len(pairs)),
        in_specs=[pl.BlockSpec((tq, D_HEADS * 2 * LANES), lambda b, s, qt, kt: (b * nq + qt[s], 0)),
                  pl.BlockSpec((tk, 2 * LANES), lambda b, s, qt, kt: (b * nk + kt[s], 0)),
                  pl.BlockSpec((2 * LANES, tk), lambda b, s, qt, kt: (0, b * nk + kt[s]))],
        out_specs=pl.BlockSpec((tq, D_HEADS * KV_LORA), lambda b, s, qt, kt: (b * nq + qt[s], 0)),
        scratch_shapes=[pltpu.VMEM((D_HEADS, tq, 1), F32), pltpu.VMEM((D_HEADS, tq, 2 * LANES), F32)])
    return pl.pallas_call(
        functools.partial(_attn_prompt_kernel, tq=tq, tk=tk),
        grid_spec=grid_spec,
        out_shape=jax.ShapeDtypeStruct((nb * seq, D_HEADS * KV_LORA), BF16),
        compiler_params=_cparams(("arbitrary", "arbitrary")),
        name="attn_prompt",
    )(qi_tbl, kj_tbl, qcat, kcat, kcat_t)


def _attn_paged_kernel(pt_ref, q_ref, knew_ref, ckv_hbm, kpe_hbm, o_ref, ckv_buf, kpe_buf, sems, q_s, m_s, l_s,
                       acc, *, ls, gp):
    b, j = pl.program_id(0), pl.program_id(1)
    nbatch, ngroups = pl.num_programs(0), pl.num_programs(1)
    step = b * ngroups + j
    slot = step % 2

    def page_copies(bb, jj, sl):
        out = []
        for g in range(gp):
            page = pt_ref[bb, jj * gp + g]
            toks = pl.ds(g * PAGE_SIZE, PAGE_SIZE)
            out.append(pltpu.make_async_copy(ckv_hbm.at[page], ckv_buf.at[sl, toks], sems.at[sl]))
            out.append(pltpu.make_async_copy(kpe_hbm.at[page], kpe_buf.at[sl, :, toks], sems.at[sl]))
        return out

    @pl.when(step == 0)
    def _():
        for c in page_copies(b, j, slot):
            c.start()

    @pl.when(step + 1 < nbatch * ngroups)
    def _():
        nxt = step + 1
        for c in page_copies(nxt // ngroups, nxt % ngroups, 1 - slot):
            c.start()

    rows = D_HEADS * ls

    @pl.when(j == 0)
    def _():
        qc = q_ref[0]
        q_s[...] = jnp.concatenate(
            [qc[:, h * 2 * LANES:(h + 1) * 2 * LANES] for h in range(D_HEADS)], axis=0).astype(BF16)
        m_s[...] = jnp.full(m_s.shape, NEG_INF, F32)
        l_s[...] = jnp.zeros(l_s.shape, F32)
        acc[...] = jnp.zeros(acc.shape, F32)

    for c in page_copies(b, j, slot):
        c.wait()

    nt = (((1,), (1,)), ((), ()))
    q_lat = q_s[:, :KV_LORA]
    q_pe = q_s[:, KV_LORA:KV_LORA + D_ROPE]

    def online(i, s, vals):
        m_old = m_s[i]
        m_new = jnp.maximum(m_old, jnp.max(s, axis=-1, keepdims=True))
        p = jnp.exp2((s - m_new) * EXP2_SCALE)
        corr = jnp.exp2((m_old - m_new) * EXP2_SCALE)
        l_s[i] = l_s[i] * corr + jnp.sum(p, axis=-1, keepdims=True)
        acc[i] = acc[i] * corr + jnp.dot(p.astype(BF16), vals, preferred_element_type=F32)
        m_s[i] = m_new

    part = gp * PAGE_SIZE // DECODE_SPLIT
    for i in range(DECODE_SPLIT):
        ck = ckv_buf[slot, i * part:(i + 1) * part, :].astype(BF16)
        kp_t = kpe_buf[slot, :, i * part:(i + 1) * part].astype(BF16)
        online(i, lax.dot_general(q_lat, ck, nt, preferred_element_type=F32)
               + jnp.dot(q_pe, kp_t, preferred_element_type=F32), ck)

    @pl.when(j == ngroups - 1)
    def _():
        kn = knew_ref[0]
        pad = jnp.concatenate([kn, jnp.zeros((LANES - ls, 2 * LANES), F32)], axis=0).astype(BF16)
        sn = lax.dot_general(q_s[...], pad, nt, preferred_element_type=F32)
        tq_pos = lax.broadcasted_iota(jnp.int32, (rows, LANES), 0) % ls
        tk_pos = lax.broadcasted_iota(jnp.int32, (rows, LANES), 1)
        online(0, jnp.where(tq_pos >= tk_pos, sn, NEG_INF), pad[:, :KV_LORA])
        m_all = m_s[0]
        for i in range(1, DECODE_SPLIT):
            m_all = jnp.maximum(m_all, m_s[i])
        num, den = jnp.zeros((rows, KV_LORA), F32), jnp.zeros((rows, 1), F32)
        for i in range(DECODE_SPLIT):
            w = jnp.exp2((m_s[i] - m_all) * EXP2_SCALE)
            num, den = num + acc[i] * w, den + l_s[i] * w
        o = num / den
        o_ref[0] = jnp.concatenate([o[h * ls:(h + 1) * ls] for h in range(D_HEADS)], axis=1)


def _attn_paged(page_table, qcat3, knew3, cache_ckv, cache_kpe_t, gp):
    nbatch, ls = qcat3.shape[0], qcat3.shape[1]
    n_pages = page_table.shape[1]
    rows = D_HEADS * ls
    grid_spec = pltpu.PrefetchScalarGridSpec(
        num_scalar_prefetch=1,
        grid=(nbatch, n_pages // gp),
        in_specs=[pl.BlockSpec((1, ls, D_HEADS * 2 * LANES), lambda b, j, pt: (b, 0, 0)),
                  pl.BlockSpec((1, ls, 2 * LANES), lambda b, j, pt: (b, 0, 0)),
                  pl.BlockSpec(memory_space=pl.ANY), pl.BlockSpec(memory_space=pl.ANY)],
        out_specs=pl.BlockSpec((1, ls, D_HEADS * KV_LORA), lambda b, j, pt: (b, 0, 0)),
        scratch_shapes=[pltpu.VMEM((2, gp * PAGE_SIZE, KV_LORA), F32), pltpu.VMEM((2, D_ROPE, gp * PAGE_SIZE), F32),
                        pltpu.SemaphoreType.DMA((2,)), pltpu.VMEM((rows, 2 * LANES), BF16),
                        pltpu.VMEM((DECODE_SPLIT, rows, 1), F32), pltpu.VMEM((DECODE_SPLIT, rows, 1), F32),
                        pltpu.VMEM((DECODE_SPLIT, rows, KV_LORA), F32)])
    return pl.pallas_call(
        functools.partial(_attn_paged_kernel, ls=ls, gp=gp),
        grid_spec=grid_spec,
        out_shape=jax.ShapeDtypeStruct((nbatch, ls, D_HEADS * KV_LORA), F32),
        compiler_params=_cparams(("arbitrary", "arbitrary")),
        name="attn_paged",
    )(page_table, qcat3, knew3, cache_ckv, cache_kpe_t)


def _odd_out_kernel(x_ref, yc_ref, ol_ref, wuv_ref, wo_ref, lg_ref, lb_ref, o_ref):
    yd = _dot(ol_ref[...], wuv_ref[...])
    y = _dot(jnp.concatenate([yc_ref[...], yd], axis=1), wo_ref[...])
    o_ref[...] = _layernorm(ALPHA * x_ref[...] + y, lg_ref[...], lb_ref[...])


def _odd_out(x2d, yc, olat, tm, arrs):
    rows = x2d.shape[0]
    row_spec = lambda w: pl.BlockSpec((tm, w), lambda i: (i, 0))
    return pl.pallas_call(
        _odd_out_kernel,
        grid=(rows // tm,),
        in_specs=[row_spec(D_MODEL), row_spec(C_WIDTH), row_spec(D_HEADS * KV_LORA)]
                 + [_const_spec(a.shape) for a in arrs],
        out_specs=row_spec(D_MODEL),
        out_shape=jax.ShapeDtypeStruct((rows, D_MODEL), F32),
        compiler_params=_cparams(("arbitrary",)),
        name="odd_out",
    )(x2d, yc, olat, *arrs)


def _block_diag(blocks):
    n, r, c = blocks.shape
    eye = jnp.eye(n, dtype=blocks.dtype)
    return (eye[:, None, :, None] * blocks[:, :, None, :]).reshape(n * r, n * c)


def _rope_tables(positions):
    half = D_ROPE // 2
    inv = ROPE_BASE ** (-jnp.arange(half, dtype=F32) / half)
    ang = positions[:, None] * inv
    cos, sin = jnp.cos(ang), jnp.sin(ang)
    reps = LANES // D_ROPE
    return jnp.tile(jnp.concatenate([cos, cos], axis=1), (1, reps)), jnp.tile(jnp.concatenate([-sin, sin], axis=1), (1, reps))


def _tile_rows(n, cap):
    t = min(n, cap)
    assert n % t == 0, (n, cap)
    return t


def kernel(x_prompt, x_sample, state_pool_buf, state_rwkv, state_shift, cache_ckv, cache_kpe, page_table, ln_g, ln_b, w_in_even, pool_w, pool_scale, sgu_ln_g, sgu_ln_b, sgu_w, sgu_b, w_out_even, w_in_odd, rwkv_mu, rwkv_w0, rwkv_w2, rwkv_a0, rwkv_a2, rwkv_g2, rwkv_kk, rwkv_ka, rwkv_rk, rwkv_gn_g, rwkv_gn_b, mla_gq, mla_gkv, mla_wuq, mla_wuk, mla_wuv, w_out_odd, w_router, router_bias, w_gate, w_up, w_down):
    bp, lp, _ = x_prompt.shape
    bs, ls, _ = x_sample.shape
    n_pages = page_table.shape[1]
    past_len = n_pages * PAGE_SIZE
    assert ls == 8 and lp % SGU_CHUNK == 0 and lp % RWKV_CHUNK == 0
    rows_p, rows_s = bp * lp, bs * ls
    tm_p = _tile_rows(lp, 512)
    tm_s = _tile_rows(rows_s, 512)
    tm_moe_p = _tile_rows(rows_p, 512)
    tm_moe_s = _tile_rows(rows_s, 512)

    xp = x_prompt.reshape(rows_p, D_MODEL)
    xs = x_sample.reshape(rows_s, D_MODEL)

    wr_pad = jnp.pad(w_router, ((0, 0), (0, LANES - N_EXPERTS)))
    rb_pad = jnp.pad(router_bias, (0, LANES - N_EXPERTS)).reshape(1, LANES)
    wg, wu, wd = w_gate.astype(BF16), w_up.astype(BF16), w_down.astype(BF16)

    outs = {}
    for layer in range(DEPTH):
        i = layer // 2
        g1, b1 = ln_g[layer, 0], ln_b[layer, 0]
        if layer % 2 == 0:
            wi_split, wo_split = _split_weight(w_in_even[i]), _split_weight(w_out_even[i])
            wts_p = _even_weights(wi_split, pool_w[i], pool_scale[i], sgu_ln_g[i], sgu_ln_b[i], sgu_w[i],
                                  sgu_b[i].T, wo_split, g1, b1)
            reps = SGU_CHUNK // ls
            w_small = jnp.stack([_block_diag(jnp.broadcast_to(sgu_w[i][h, :ls, :ls], (reps, ls, ls)))
                                 for h in range(B_HEADS)])
            b_small = jnp.tile(sgu_b[i][:, :ls], (1, reps)).T
            wts_s = _even_weights(wi_split, pool_w[i], pool_scale[i], sgu_ln_g[i], sgu_ln_b[i], w_small,
                                  b_small, wo_split, g1, b1)
            xp, tail = _even_prompt(xp, bp, lp, _tile_rows(lp, 256), wts_p)
            bufpad = jnp.pad(state_pool_buf[i], ((0, 0), (HALO - POOL_BUF, 0), (0, 0)))
            xs, a_new, vn_s = _even_sample(xs, bufpad, ls, tm_s, wts_s)
            outs.setdefault("pool_p", []).append(tail.reshape(bp, HALO, A_WIDTH)[:, HALO - POOL_BUF:])
            ext = jnp.concatenate([state_pool_buf[i], a_new.reshape(bs, ls, A_WIDTH)], axis=1)
            outs.setdefault("pool_s", []).append(ext[:, -POOL_BUF:])
            outs.setdefault("sgu_s", []).append(vn_s.reshape(bs, ls, B_WIDTH))
        else:
            nope = D_HEADS * D_NOPE
            wq_cols = mla_wuq[i].reshape(Q_LORA, D_HEADS, D_NOPE + D_ROPE)
            wuq_perm = jnp.concatenate([wq_cols[:, :, :D_NOPE].reshape(Q_LORA, nope),
                                        wq_cols[:, :, D_NOPE:].reshape(Q_LORA, D_HEADS * D_ROPE)], axis=1)
            wuk_bd = _block_diag(jnp.swapaxes(mla_wuk[i], 1, 2))
            wuv_bd = _block_diag(mla_wuv[i])
            w_in = w_in_odd[i]
            w_in_pad = jnp.concatenate([w_in[:, :C_COLS + Q_LORA + KV_LORA],
                                        jnp.tile(w_in[:, C_COLS + Q_LORA + KV_LORA:], (1, LANES // D_ROPE))], axis=1)
            in_arrs = [w_in_pad.astype(BF16), mla_gq[i].reshape(1, -1), mla_gkv[i].reshape(1, -1),
                       wuq_perm.astype(BF16), wuk_bd.astype(BF16)]
            cos_p, sin_p = _rope_tables(jnp.arange(lp, dtype=F32))
            cos_s, sin_s = _rope_tables(past_len + jnp.arange(ls, dtype=F32))
            cos_s, sin_s = jnp.tile(cos_s, (tm_s // ls, 1)), jnp.tile(sin_s, (tm_s // ls, 1))
            pc_p, qcat_p, ckv_p, kpe_p, kcat_p, kcat_t_p = _odd_in(xp, cos_p, sin_p, tm_p, lp // tm_p, in_arrs, BF16)
            pc_s, qcat_s, ckv_s, kpe_s, kcat_s, _ = _odd_in(xs, cos_s, sin_s, tm_s, 1, in_arrs, F32)

            z64 = jnp.zeros((C_DECAY_LORA, C_WIDTH), F32)
            wa2 = jnp.concatenate([jnp.concatenate([rwkv_w2[i], z64], axis=1),
                                   jnp.concatenate([z64, rwkv_a2[i]], axis=1)], axis=0)
            row = lambda v: v.reshape(1, -1)
            rw_arrs = [row(rwkv_mu[i]), row(rwkv_w0[i]), wa2.astype(BF16), row(rwkv_a0[i]), rwkv_g2[i].astype(BF16),
                       row(rwkv_kk[i]), row(rwkv_ka[i]), row(rwkv_rk[i]), row(rwkv_gn_g[i]), row(rwkv_gn_b[i]),
                       _block_diag(jnp.ones((C_HEADS, C_HEAD_DIM, C_HEAD_DIM), BF16))]
            zero_state = jnp.zeros((bp, C_HEADS, C_HEAD_DIM, C_HEAD_DIM), F32)
            yc_p, st_p = _rwkv(pc_p, jnp.zeros((bp, 1, C_COLS), F32), _state_to_groups(zero_state),
                               bp, lp, RWKV_CHUNK, RWKV_BLOCK, rw_arrs)
            yc_s, st_s = _rwkv(pc_s, state_shift[i], _state_to_groups(state_rwkv[i]), bs, ls, RWKV_CHUNK_SHORT, 1,
                               rw_arrs)

            ol_p = _attn_prompt(qcat_p, kcat_p, kcat_t_p, bp, lp, _tile_rows(lp, 256), _tile_rows(lp, 512))
            ol_s = _attn_paged(page_table, qcat_s.reshape(bs, ls, -1), kcat_s.reshape(bs, ls, -1),
                               cache_ckv[i], jnp.swapaxes(cache_kpe[i], 1, 2), math.gcd(n_pages, PAGES_PER_STEP))

            out_arrs = [wuv_bd.astype(BF16), w_out_odd[i].astype(BF16), g1.reshape(1, -1), b1.reshape(1, -1)]
            xp_new = _odd_out(xp, yc_p, ol_p, tm_p, out_arrs)
            xs_new = _odd_out(xs, yc_s, ol_s.reshape(rows_s, -1), tm_s, out_arrs)

            outs.setdefault("rwkv_p", []).append(_groups_to_state(st_p))
            outs.setdefault("rwkv_s", []).append(_groups_to_state(st_s))
            outs.setdefault("shift_p", []).append(pc_p.reshape(bp, lp, C_COLS)[:, -1:])
            outs.setdefault("shift_s", []).append(pc_s.reshape(bs, ls, C_COLS)[:, -1:])
            outs.setdefault("ckv_p", []).append(ckv_p.reshape(bp, lp, KV_LORA))
            outs.setdefault("ckv_s", []).append(ckv_s.reshape(bs, ls, KV_LORA))
            outs.setdefault("kpe_p", []).append(kpe_p.reshape(bp, lp, D_ROPE))
            outs.setdefault("kpe_s", []).append(kpe_s.reshape(bs, ls, D_ROPE))
            xp, xs = xp_new, xs_new
        g2, b2 = ln_g[layer, 1], ln_b[layer, 1]
        xp = _moe(xp, tm_moe_p, wr_pad, rb_pad, wg[layer], wu[layer], wd[layer], g2, b2)
        xs = _moe(xs, tm_moe_s, wr_pad, rb_pad, wg[layer], wu[layer], wd[layer], g2, b2)

    st = lambda k: jnp.stack(outs[k])
    return (xp.reshape(bp, lp, D_MODEL), xs.reshape(bs, ls, D_MODEL), st("pool_p"), st("pool_s"), st("sgu_s"),
            st("rwkv_p"), st("rwkv_s"), st("shift_p"), st("shift_s"), st("ckv_p"), st("ckv_s"), st("kpe_p"),
            st("kpe_s"))
```

```python
import functools
import math

import jax
import jax.numpy as jnp
import numpy as np
from jax import lax
from jax.experimental import pallas as pl
from jax.experimental.pallas import tpu as pltpu

F32 = jnp.float32
BF16 = jnp.bfloat16

D_MODEL = 1024
DEPTH = 2
ALPHA = (2 * DEPTH) ** 0.25
LN_EPS = 1e-5
RMS_EPS = 1e-6
A_WIDTH = 512
POOL_WINDOWS = (2, 4, 8, 16)
HALO = 16
POOL_BUF = 15
B_WIDTH = 512
B_HEADS = 4
SGU_CHUNK = 128
C_WIDTH = 512
C_HEAD_DIM = 64
C_HEADS = 8
C_COLS = 1792
RWKV_GN_EPS = 64e-5
C_DECAY_LORA = 64
RWKV_CHUNK = 64
RWKV_BLOCK = 4
RWKV_CHUNK_SHORT = 16
RWKV_GROUP = 4
D_HEADS = 8
D_NOPE = 64
D_ROPE = 32
Q_LORA = 256
KV_LORA = 128
MLA_SCALE = (D_NOPE + D_ROPE) ** -0.5
ROPE_BASE = 10000.0
ONES_LANE = D_ROPE
EXP2_SCALE = MLA_SCALE * math.log2(math.e)
PAGE_SIZE = 128
PAGES_PER_STEP = 64
DECODE_SPLIT = 1
N_EXPERTS = 16
D_EXPERT = 256
LANES = 128
VMEM_LIMIT = 56 * 1024 * 1024

NEG_INF = float("-inf")


def _cparams(sem):
    return pltpu.CompilerParams(dimension_semantics=sem, vmem_limit_bytes=VMEM_LIMIT)


def _dot(a, b):
    return jnp.dot(a.astype(BF16), b.astype(BF16), preferred_element_type=F32)


def _dot_nt(a, b):
    return lax.dot_general(a.astype(BF16), b.astype(BF16), (((1,), (1,)), ((), ())), preferred_element_type=F32)


def _dot_tn(a, b):
    return lax.dot_general(a.astype(BF16), b.astype(BF16), (((0,), (0,)), ((), ())), preferred_element_type=F32)


def _split3(a):
    a1 = a.astype(BF16)
    r1 = a - a1.astype(F32)
    a2 = r1.astype(BF16)
    a3 = (r1 - a2.astype(F32)).astype(BF16)
    return a1, a2, a3


def _split2(a):
    hi = a.astype(BF16)
    return hi, (a - hi.astype(F32)).astype(BF16)


def _dot3(a, b_hi, b_lo):
    a_hi, a_lo = _split2(a)
    return (jnp.dot(a_hi, b_hi, preferred_element_type=F32)
            + (jnp.dot(a_hi, b_lo, preferred_element_type=F32) + jnp.dot(a_lo, b_hi, preferred_element_type=F32)))


def _layernorm(r, g, b):
    mu = jnp.mean(r, axis=-1, keepdims=True)
    d = r - mu
    var = jnp.mean(d * d, axis=-1, keepdims=True)
    return d * lax.rsqrt(var + LN_EPS) * g + b


def _rmsnorm(x, g):
    return x * lax.rsqrt(jnp.mean(x * x, axis=-1, keepdims=True) + RMS_EPS) * g


def _gelu(x):
    return 0.5 * x * (1.0 + lax.erf(x * np.float32(math.sqrt(0.5))))


def _even_tail(x, z, pooled, wrefs, tm):
    pw_ref, ps_ref, sg_ref, sb_ref, sw_ref, sbc_ref, woh_ref, wol_ref, lg_ref, lb_ref = wrefs
    u, v = z[:, :B_WIDTH], z[:, B_WIDTH:]
    a_out = jnp.concatenate(
        [_dot3(pooled[g], *_split2(pw_ref[g])) for g in range(len(POOL_WINDOWS))], axis=1) * ps_ref[...]
    row = lax.broadcasted_iota(jnp.int32, (SGU_CHUNK, SGU_CHUNK), 0)
    col = lax.broadcasted_iota(jnp.int32, (SGU_CHUNK, SGU_CHUNK), 1)
    vn_parts, b_parts = [], []
    for h in range(B_HEADS):
        sl = slice(h * LANES, (h + 1) * LANES)
        vh = v[:, sl]
        mu = jnp.mean(vh, axis=-1, keepdims=True)
        d = vh - mu
        var = jnp.mean(d * d, axis=-1, keepdims=True)
        vn = d * lax.rsqrt(var + LN_EPS) * sg_ref[:, sl] + sb_ref[:, sl]
        vn_parts.append(vn)
        wm = jnp.where(col <= row, sw_ref[h], 0.0)
        bias = sbc_ref[:, h:h + 1]
        v_hi, v_lo = _split2(vn)
        mix = jnp.concatenate(
            [_dot3(wm, v_hi[n * SGU_CHUNK:(n + 1) * SGU_CHUNK], v_lo[n * SGU_CHUNK:(n + 1) * SGU_CHUNK]) + bias
             for n in range(tm // SGU_CHUNK)], axis=0)
        b_parts.append(u[:, sl] * mix)
    vn_all = jnp.concatenate(vn_parts, axis=1)
    cat = jnp.concatenate([a_out] + b_parts, axis=1)
    y = _dot3(cat, woh_ref[...], wol_ref[...])
    return _layernorm(ALPHA * x + y, lg_ref[...], lb_ref[...]), vn_all


def _even_prompt_kernel(x_ref, wih_ref, wil_ref, *rest, tm):
    wrefs, (xo_ref, tail_ref, abuf) = rest[:-3], rest[-3:]
    t = pl.program_id(1)
    x = x_ref[...]
    p = _dot3(x, wih_ref[...], wil_ref[...])
    a = p[:, :A_WIDTH]
    z = _gelu(p[:, A_WIDTH:])

    @pl.when(t == 0)
    def _():
        abuf[0:HALO, :] = jnp.zeros((HALO, A_WIDTH), F32)

    abuf[HALO:HALO + tm, :] = a
    pos = t * tm + lax.broadcasted_iota(jnp.int32, (tm, LANES), 0)
    pooled = []
    for g, w in enumerate(POOL_WINDOWS):
        sl = slice(g * LANES, (g + 1) * LANES)
        acc = a[:, sl]
        for s in range(1, w):
            acc = acc + abuf[HALO - s:HALO - s + tm, sl]
        cnt = jnp.minimum(w, pos + 1).astype(F32)
        pooled.append(acc / cnt - a[:, sl])
    tail = abuf[tm:tm + HALO, :]
    abuf[0:HALO, :] = tail
    tail_ref[...] = tail
    xo, _ = _even_tail(x, z, pooled, wrefs, tm)
    xo_ref[...] = xo


def _even_sample_kernel(x_ref, buf_ref, wih_ref, wil_ref, *rest, tm, ls):
    wrefs, (xo_ref, anew_ref, vn_ref, ext) = rest[:-4], rest[-4:]
    nseq = tm // ls
    x = x_ref[...]
    p = _dot3(x, wih_ref[...], wil_ref[...])
    a = p[:, :A_WIDTH]
    z = _gelu(p[:, A_WIDTH:])
    ext[:, 0:HALO, :] = buf_ref[...]
    ext[:, HALO:HALO + ls, :] = a.reshape(nseq, ls, A_WIDTH)
    pooled = []
    for g, w in enumerate(POOL_WINDOWS):
        sl = slice(g * LANES, (g + 1) * LANES)
        acc = ext[:, HALO:HALO + ls, sl]
        for s in range(1, w):
            acc = acc + ext[:, HALO - s:HALO - s + ls, sl]
        pooled.append(acc.reshape(tm, LANES) * np.float32(1.0 / w) - a[:, sl])
    xo, vn = _even_tail(x, z, pooled, wrefs, tm)
    xo_ref[...] = xo
    anew_ref[...] = a
    vn_ref[...] = vn


def _const_spec(shape):
    nd = len(shape)
    return pl.BlockSpec(shape, lambda *_: (0,) * nd)


def _split_weight_kernel(w_ref, hi_ref, lo_ref):
    hi, lo = _split2(w_ref[...])
    hi_ref[...] = hi
    lo_ref[...] = lo


def _split_weight(w):
    rows, cols = w.shape
    tr = _tile_rows(rows, 256)
    spec = pl.BlockSpec((tr, cols), lambda i: (i, 0))
    return pl.pallas_call(
        _split_weight_kernel,
        grid=(rows // tr,),
        in_specs=[spec],
        out_specs=[spec, spec],
        out_shape=[jax.ShapeDtypeStruct(w.shape, BF16)] * 2,
        compiler_params=_cparams(("arbitrary",)),
        name="split_weight",
    )(w)


def _even_weights(w_in_split, pool_w, pool_scale, sgu_g, sgu_b, sgu_w, sgu_bcol, w_out_split, ln_g, ln_b):
    arrs = [*w_in_split, pool_w, pool_scale.reshape(1, -1), sgu_g.reshape(1, -1),
            sgu_b.reshape(1, -1), sgu_w, sgu_bcol, *w_out_split, ln_g.reshape(1, -1), ln_b.reshape(1, -1)]
    return arrs, [_const_spec(a.shape) for a in arrs]


def _even_prompt(x2d, nb, seq, tm, wts):
    nt = seq // tm
    arrs, specs = wts
    row_spec = lambda w: pl.BlockSpec((tm, w), lambda b, t: (b * nt + t, 0))
    return pl.pallas_call(
        functools.partial(_even_prompt_kernel, tm=tm),
        grid=(nb, nt),
        in_specs=[row_spec(D_MODEL)] + specs,
        out_specs=[row_spec(D_MODEL), pl.BlockSpec((HALO, A_WIDTH), lambda b, t: (b, 0))],
        out_shape=[jax.ShapeDtypeStruct((nb * seq, D_MODEL), F32), jax.ShapeDtypeStruct((nb * HALO, A_WIDTH), F32)],
        scratch_shapes=[pltpu.VMEM((tm + HALO, A_WIDTH), F32)],
        compiler_params=_cparams(("arbitrary", "arbitrary")),
        name="even_prompt",
    )(x2d, *arrs)


def _even_sample(x2d, bufpad, ls, tm, wts):
    rows = x2d.shape[0]
    nseq = tm // ls
    arrs, specs = wts
    row_spec = lambda w: pl.BlockSpec((tm, w), lambda i: (i, 0))
    return pl.pallas_call(
        functools.partial(_even_sample_kernel, tm=tm, ls=ls),
        grid=(rows // tm,),
        in_specs=[row_spec(D_MODEL), pl.BlockSpec((nseq, HALO, A_WIDTH), lambda i: (i, 0, 0))] + specs,
        out_specs=[row_spec(D_MODEL), row_spec(A_WIDTH), row_spec(B_WIDTH)],
        out_shape=[jax.ShapeDtypeStruct((rows, D_MODEL), F32), jax.ShapeDtypeStruct((rows, A_WIDTH), F32),
                   jax.ShapeDtypeStruct((rows, B_WIDTH), F32)],
        scratch_shapes=[pltpu.VMEM((nseq, HALO + ls, A_WIDTH), F32)],
        compiler_params=_cparams(("arbitrary",)),
        name="even_sample",
    )(x2d, bufpad, *arrs)


def _route(scores, bias):
    shape = scores.shape
    lane = lax.broadcasted_iota(jnp.int32, shape, 1)
    pos_in_group = lane % 4
    group = lane // 4
    biased = scores + bias

    def from_lane(x, d):
        return pltpu.roll(x, d % LANES, 1)

    offsets = (-3, -2, -1, 1, 2, 3)
    rank = jnp.zeros(shape, jnp.int32)
    for d in offsets:
        src = pos_in_group - d
        valid = (src >= 0) & (src <= 3)
        other = from_lane(biased, d)
        beats = (other > biased) | ((other == biased) & (d > 0))
        rank = rank + jnp.where(valid & beats, 1, 0)
    top2 = rank < 2
    kept = jnp.where(top2, biased, 0.0)
    gsum = kept
    for d in offsets:
        src = pos_in_group - d
        valid = (src >= 0) & (src <= 3)
        gsum = gsum + jnp.where(valid, from_lane(kept, d), 0.0)
    lost = jnp.zeros(shape, jnp.int32)
    for dg in offsets:
        src = group - dg
        valid = (src >= 0) & (src <= 3)
        other = from_lane(gsum, 4 * dg)
        beats = (other > gsum) | ((other == gsum) & (dg > 0))
        lost = lost + jnp.where(valid & beats, 1, 0)
    selected = top2 & (lost == 0) & (lane < N_EXPERTS)
    picked = jnp.where(selected, scores, 0.0)
    return picked / jnp.sum(picked, axis=-1, keepdims=True)


def _moe_kernel(x_ref, wr_ref, rb_ref, wg_ref, wu_ref, wd_ref, lg_ref, lb_ref, o_ref, hbuf):
    x = x_ref[...]
    x1 = x.astype(BF16)
    x2 = (x - x1.astype(F32)).astype(BF16)
    wr = wr_ref[...]
    w1 = wr.astype(BF16)
    w2 = (wr - w1.astype(F32)).astype(BF16)
    hi_part = jnp.dot(x1, jnp.concatenate([w1, w2], axis=1), preferred_element_type=F32)
    logits = hi_part[:, :LANES] + (hi_part[:, LANES:] + jnp.dot(x2, w1, preferred_element_type=F32))
    gates = _route(jax.nn.sigmoid(logits), rb_ref[...])
    for e in range(N_EXPERTS):
        hg = jnp.dot(x1, wg_ref[e], preferred_element_type=F32)
        hu = jnp.dot(x1, wu_ref[e], preferred_element_type=F32)
        he = hg * jax.nn.sigmoid(hg) * hu * gates[:, e:e + 1]
        hbuf[:, e * D_EXPERT:(e + 1) * D_EXPERT] = he.astype(BF16)
    y = jnp.dot(hbuf[...], wd_ref[...], preferred_element_type=F32)
    o_ref[...] = _layernorm(ALPHA * x + y, lg_ref[...], lb_ref[...])


def _resident_spec(shape):
    nd = len(shape)
    return pl.BlockSpec(shape, lambda *_: (0,) * nd, pipeline_mode=pl.Buffered(1))


def _moe(x2d, tm, wr_pad, rb_pad, wg, wu, wd, ln_g, ln_b):
    rows = x2d.shape[0]
    row_spec = pl.BlockSpec((tm, D_MODEL), lambda i: (i, 0))
    wd_all = wd.reshape(N_EXPERTS * D_EXPERT, D_MODEL)
    return pl.pallas_call(
        _moe_kernel,
        grid=(rows // tm,),
        in_specs=[row_spec, _const_spec(wr_pad.shape), _const_spec(rb_pad.shape),
                  _resident_spec(wg.shape), _resident_spec(wu.shape), _resident_spec(wd_all.shape),
                  _const_spec((1, D_MODEL)), _const_spec((1, D_MODEL))],
        out_specs=row_spec,
        out_shape=jax.ShapeDtypeStruct((rows, D_MODEL), F32),
        scratch_shapes=[pltpu.VMEM((tm, N_EXPERTS * D_EXPERT), BF16)],
        compiler_params=_cparams(("arbitrary",)),
        name="moe",
    )(x2d, wr_pad, rb_pad, wg, wu, wd_all, ln_g.reshape(1, -1), ln_b.reshape(1, -1))


def _rope_lanes(x, cos_t, sin_t):
    lane = lax.broadcasted_iota(jnp.int32, x.shape, 1)
    partner = jnp.where((lane % D_ROPE) < D_ROPE // 2,
                        pltpu.roll(x, LANES - D_ROPE // 2, 1), pltpu.roll(x, D_ROPE // 2, 1))
    return x * cos_t + partner * sin_t


def _key_tail(kpe_lanes):
    lane = lax.broadcasted_iota(jnp.int32, kpe_lanes.shape, 1)
    return jnp.where(lane < D_ROPE, kpe_lanes, jnp.where(lane == ONES_LANE, 1.0, 0.0))


def _odd_in_kernel(x_ref, cos_ref, sin_ref, wi_ref, gq_ref, gkv_ref, wuq_ref, wuk_ref,
                   pc_ref, qcat_ref, ckv_ref, kpe_ref, kcat_ref, kcat_t_ref):
    p = _dot(x_ref[...], wi_ref[...])
    pc_ref[...] = p[:, :C_COLS]
    o = C_COLS
    cq = _rmsnorm(p[:, o:o + Q_LORA], gq_ref[...])
    ckv = _rmsnorm(p[:, o + Q_LORA:o + Q_LORA + KV_LORA], gkv_ref[...])
    cos_t, sin_t = cos_ref[...], sin_ref[...]
    kslot = _rope_lanes(p[:, o + Q_LORA + KV_LORA:o + Q_LORA + KV_LORA + LANES], cos_t, sin_t)
    q = _dot(cq, wuq_ref[...])
    nope = D_HEADS * D_NOPE
    qlat = _dot(q[:, :nope], wuk_ref[...])
    lane = lax.broadcasted_iota(jnp.int32, kslot.shape, 1)
    per_vreg = LANES // D_ROPE
    for i in range(D_HEADS // per_vreg):
        qpe = _rope_lanes(q[:, nope + i * LANES:nope + (i + 1) * LANES], cos_t, sin_t)
        for j in range(per_vreg):
            h = i * per_vreg + j
            pe = qpe if j == 0 else pltpu.roll(qpe, LANES - j * D_ROPE, 1)
            row = jnp.concatenate([qlat[:, h * KV_LORA:(h + 1) * KV_LORA], jnp.where(lane < D_ROPE, pe, 0.0)], axis=1)
            qcat_ref[:, h * 2 * LANES:(h + 1) * 2 * LANES] = row.astype(qcat_ref.dtype)
    ckv_ref[...] = ckv
    kpe_ref[...] = kslot[:, :D_ROPE]
    kcat = jnp.concatenate([ckv, _key_tail(kslot)], axis=1)
    kcat_ref[...] = kcat.astype(kcat_ref.dtype)
    kcat_t_ref[...] = kcat.T.astype(kcat_t_ref.dtype)


def _odd_in(x2d, cos_t, sin_t, tm, pos_blocks, arrs, qdt):
    rows = x2d.shape[0]
    row_spec = lambda w: pl.BlockSpec((tm, w), lambda i: (i, 0))
    pos_spec = pl.BlockSpec((tm, LANES), lambda i: (i % pos_blocks, 0))
    out_w = [(C_COLS, F32), (D_HEADS * 2 * LANES, qdt), (KV_LORA, F32), (D_ROPE, F32), (2 * LANES, qdt)]
    return pl.pallas_call(
        _odd_in_kernel,
        grid=(rows // tm,),
        in_specs=[row_spec(D_MODEL), pos_spec, pos_spec] + [_const_spec(a.shape) for a in arrs],
        out_specs=[row_spec(w) for w, _ in out_w] + [pl.BlockSpec((2 * LANES, tm), lambda i: (0, i))],
        out_shape=[jax.ShapeDtypeStruct((rows, w), dt) for w, dt in out_w]
                  + [jax.ShapeDtypeStruct((2 * LANES, rows), qdt)],
        compiler_params=_cparams(("arbitrary",)),
        name="odd_in",
    )(x2d, cos_t, sin_t, *arrs)


def _rwkv_kernel(pc_ref, shift_ref, p0_ref, mu_ref, w0_ref, wa2_ref, a0_ref, g2_ref, kkc_ref, kac_ref, rk_ref,
                 gng_ref, gnb_ref, ho_ref, y_ref, pout_ref, state, xbuf, *, chunk, n_blk, n_valid):
    C = chunk
    HG = RWKV_GROUP
    GR, GL = HG * C, HG * C_HEAD_DIM
    R = n_blk * C
    n_in = R if n_valid == C else n_valid
    n = pl.program_id(1)

    @pl.when(n == 0)
    def _():
        state[...] = p0_ref[0]
        xbuf[7:8, :] = shift_ref[0]

    pc = pc_ref[...].reshape(n_in, C_COLS)
    if n_in < R:
        pc = jnp.concatenate([pc, jnp.zeros((R - n_in, C_COLS), F32)], axis=0)
    xbuf[8:8 + R, :] = pc
    prev = xbuf[7:7 + R, :]
    xs = pc + (prev - pc) * mu_ref[...]
    xbuf[7:8, :] = pc[n_in - 1:n_in, :]

    W = C_WIDTH
    r, k, v = xs[:, :W], xs[:, W:2 * W], xs[:, 2 * W:3 * W]
    wa = xs[:, 3 * W:3 * W + LANES]
    gl = xs[:, 3 * W + LANES:]
    first_half_r = lax.broadcasted_iota(jnp.int32, (R, LANES), 1) < C_HEAD_DIM
    lora = _dot(jnp.where(first_half_r, jnp.tanh(wa), wa), wa2_ref[...])
    zw = -(w0_ref[...] + lora[:, :W])
    softplus = jnp.maximum(zw, 0.0) + jnp.log1p(jnp.exp(-jnp.abs(zw)))
    ld = -jnp.exp(-softplus - 0.5)
    a = jax.nn.sigmoid(a0_ref[...] + lora[:, W:])
    g = _dot(jax.nn.sigmoid(gl), g2_ref[...])

    head_ones = ho_ref[...]

    def head_sum(xv):
        return _dot(xv, head_ones)

    kk = k * kkc_ref[...]
    kp = k * (1.0 + (a - 1.0) * kac_ref[...])
    sums = head_sum(jnp.concatenate([kk * kk, r * kp * rk_ref[...]], axis=0))
    kk = kk * lax.rsqrt(sums[:R] + 1e-12)
    bonus = sums[R:] * v
    if n_in < R:
        valid = lax.broadcasted_iota(jnp.int32, (R, W), 0) < n_in
        ld = jnp.where(valid, ld, 0.0)
        kk = jnp.where(valid, kk, 0.0)
        kp = jnp.where(valid, kp, 0.0)
        v = jnp.where(valid, v, 0.0)
    b = kk * a

    tr = lax.broadcasted_iota(jnp.int32, (R, R), 0)
    tc = lax.broadcasted_iota(jnp.int32, (R, R), 1)
    tri = jnp.where(((tr // C) == (tc // C)) & (tc <= tr), 1.0, 0.0).astype(BF16)
    lam3 = jnp.dot(tri, jnp.concatenate(_split3(ld), axis=1), preferred_element_type=F32)
    lam = lam3[:, :W] + lam3[:, W:2 * W] + lam3[:, 2 * W:]
    lam_ends = [lam[c * C + C - 1:c * C + C, :] for c in range(n_blk)]
    lam_end = jnp.concatenate([jnp.broadcast_to(le, (C, W)) for le in lam_ends], axis=0)
    e_neg = jnp.exp(-lam)
    e_end = jnp.exp(lam_end - lam)
    rt = r * jnp.exp(lam)
    kt = kp * e_neg
    kkt = kk * jnp.exp(lam - ld)
    bt = b * e_neg
    kg = kp * e_end
    bg = b * e_end

    r2 = lax.broadcasted_iota(jnp.int32, (GR, GR), 0)
    c2 = lax.broadcasted_iota(jnp.int32, (GR, GR), 1)
    same_blk = (r2 // C) == (c2 // C)
    strict = same_blk & ((c2 % C) < (r2 % C))
    incl = same_blk & ((c2 % C) <= (r2 % C))
    eye_r = jnp.where(r2 == c2, 1.0, 0.0)
    eye_l = (lax.broadcasted_iota(jnp.int32, (GL, GL), 0) == lax.broadcasted_iota(jnp.int32, (GL, GL), 1))
    head_of_lane = lax.broadcasted_iota(jnp.int32, (C, GL), 1) // C_HEAD_DIM

    def stack(xv, c, q):
        xq = xv[c * C:(c + 1) * C, q * GL:(q + 1) * GL]
        return jnp.concatenate([jnp.where(head_of_lane == h, xq, 0.0) for h in range(HG)], axis=0)

    def unstack(xz):
        out = xz[:C]
        for h in range(1, HG):
            out = out + xz[h * C:(h + 1) * C]
        return out

    ngroup = W // GL
    pre = []
    for c, q in [(c, q) for c in range(n_blk) for q in range(ngroup)]:
        gz, rz, bz, kz, vz = stack(kkt, c, q), stack(rt, c, q), stack(bt, c, q), stack(kt, c, q), stack(v, c, q)
        bgz, kgz = stack(bg, c, q), stack(kg, c, q)
        sc = _dot_nt(jnp.concatenate([gz, rz], axis=0), jnp.concatenate([bz, kz], axis=0))
        aab = jnp.where(strict, sc[:GR, :GR], 0.0)
        aak = jnp.where(strict, sc[:GR, GR:], 0.0)
        rrb = jnp.where(incl, sc[GR:, :GR], 0.0)
        rrk = jnp.where(incl, sc[GR:, GR:], 0.0)
        tinv = eye_r - aab
        xp = _dot(aab, aab)
        for _ in range(int(math.log2(C)) - 2):
            sq = _dot(xp, jnp.concatenate([tinv, xp], axis=1))
            tinv = tinv + sq[:, :GR]
            xp = sq[:, GR:]
        tinv = tinv + _dot(xp, tinv)
        akv = _dot(aak, vz)
        wu = _dot(tinv, jnp.concatenate([gz, akv], axis=1))
        bwu = _dot_tn(bgz, wu)
        g_end_q = jnp.exp(lam_ends[c][:, q * GL:(q + 1) * GL])
        m_mat = jnp.where(eye_l, g_end_q, 0.0) - bwu[:, :GL]
        n_mat = _dot_tn(kgz, vz) - bwu[:, GL:]
        rwu = _dot(rrb, wu)
        y0 = unstack(_dot(rrk, vz) - rwu[:, GL:])
        pre.append((jnp.concatenate([unstack(rz - rwu[:, :GL]), m_mat], axis=0), y0, n_mat))

    y_rows = []
    for c in range(n_blk):
        ys = []
        for q in range(ngroup):
            qm_m, y0, n_mat = pre[c * ngroup + q]
            yp = _dot(qm_m, state[q])
            ys.append(yp[:C] + y0)
            state[q] = yp[C:] + n_mat
        y_rows.append(jnp.concatenate(ys, axis=1))
    y = jnp.concatenate(y_rows, axis=0)

    inv_n = np.float32(1.0 / C_HEAD_DIM)
    mean = head_sum(y) * inv_n
    d = y - mean
    var = head_sum(d * d) * inv_n
    yn = d * lax.rsqrt(var + RWKV_GN_EPS) * gng_ref[...] + gnb_ref[...]
    out = (yn + bonus) * g
    y_ref[...] = out[:n_in].reshape(y_ref.shape)
    pout_ref[0] = state[...]


def _rwkv(pc, shift_in, p0, nb, seq, C, n_blk, arrs):
    gl = RWKV_GROUP * C_HEAD_DIM
    ngroup = C_WIDTH // gl
    if seq >= C:
        rows_blk = C * n_blk
        assert seq % rows_blk == 0, (seq, C, n_blk)
        n_valid, n_steps = C, seq // rows_blk
        pc_in, pc_spec = pc, pl.BlockSpec((rows_blk, C_COLS), lambda b, n: (b * n_steps + n, 0))
        y_shape = (pc.shape[0], C_WIDTH)
        y_spec = pl.BlockSpec((rows_blk, C_WIDTH), lambda b, n: (b * n_steps + n, 0))
    else:
        assert n_blk == 1
        rows_blk, n_valid, n_steps = C, seq, 1
        pc_in = pc.reshape(nb, seq, C_COLS)
        pc_spec = pl.BlockSpec((1, seq, C_COLS), lambda b, n: (b, 0, 0))
        y_shape = (nb, seq, C_WIDTH)
        y_spec = pl.BlockSpec((1, seq, C_WIDTH), lambda b, n: (b, 0, 0))
    y, pout = pl.pallas_call(
        functools.partial(_rwkv_kernel, chunk=C, n_blk=n_blk, n_valid=n_valid),
        grid=(nb, n_steps),
        in_specs=[pc_spec, pl.BlockSpec((1, 1, C_COLS), lambda b, n: (b, 0, 0)),
                  pl.BlockSpec((1, ngroup, gl, gl), lambda b, n: (b, 0, 0, 0))]
                 + [_const_spec(a.shape) for a in arrs],
        out_specs=[y_spec, pl.BlockSpec((1, ngroup, gl, gl), lambda b, n: (b, 0, 0, 0))],
        out_shape=[jax.ShapeDtypeStruct(y_shape, F32), jax.ShapeDtypeStruct((nb, ngroup, gl, gl), F32)],
        scratch_shapes=[pltpu.VMEM((ngroup, gl, gl), F32), pltpu.VMEM((rows_blk + 8, C_COLS), F32)],
        compiler_params=_cparams(("arbitrary", "arbitrary")),
        name="rwkv7",
    )(pc_in, shift_in, p0, *arrs)
    return y.reshape(-1, C_WIDTH), pout


def _state_to_groups(s):
    nb, hg, n = s.shape[0], RWKV_GROUP, C_HEAD_DIM
    p = jnp.swapaxes(s, -1, -2).reshape(nb, C_HEADS // hg, hg, n, n)
    eye = jnp.eye(hg, dtype=s.dtype)
    return (eye[None, None, :, None, :, None] * p[:, :, :, :, None, :]).reshape(nb, C_HEADS // hg, hg * n, hg * n)


def _groups_to_state(p):
    nb, hg, n = p.shape[0], RWKV_GROUP, C_HEAD_DIM
    p6 = p.reshape(nb, C_HEADS // hg, hg, n, hg, n)
    s = jnp.stack([p6[:, :, h, :, h, :] for h in range(hg)], axis=2).reshape(nb, C_HEADS, n, n)
    return jnp.swapaxes(s, -1, -2)


def _softmax_step(s, kc, m_ref, acc_ref, idx):
    m_old = m_ref[idx]
    m_new = jnp.maximum(m_old, jnp.max(s, axis=-1, keepdims=True))
    p = jnp.exp2((s - jnp.tile(m_new, (1, s.shape[1] // LANES))) * EXP2_SCALE)
    corr = jnp.exp2((m_old - m_new) * EXP2_SCALE)
    acc_ref[idx] = (acc_ref[idx] * jnp.tile(corr, (1, kc.shape[1] // LANES))
                    + jnp.dot(p.astype(BF16), kc, preferred_element_type=F32))
    m_ref[idx] = m_new


def _softmax_result(acc):
    return acc[:, :KV_LORA] / acc[:, KV_LORA + ONES_LANE:KV_LORA + ONES_LANE + 1]


def _attn_prompt_kernel(qi_ref, kj_ref, q_ref, k_ref, kt_ref, o_ref, m_s, acc, *, tq, tk):
    s_id = pl.program_id(1)
    qi, kj = qi_ref[s_id], kj_ref[s_id]
    last = (qi * tq + tq - 1) // tk

    @pl.when(kj == 0)
    def _():
        m_s[...] = jnp.full(m_s.shape, NEG_INF, F32)
        acc[...] = jnp.zeros(acc.shape, F32)

    def update(masked):
        kc = k_ref[...]
        kc_t = kt_ref[...]
        if masked:
            qpos = qi * tq + lax.broadcasted_iota(jnp.int32, (tq, tk), 0)
            kpos = kj * tk + lax.broadcasted_iota(jnp.int32, (tq, tk), 1)
            visible = qpos >= kpos
        for h in range(D_HEADS):
            s = jnp.dot(q_ref[:, h * 2 * LANES:(h + 1) * 2 * LANES], kc_t, preferred_element_type=F32)
            if masked:
                s = jnp.where(visible, s, NEG_INF)
            _softmax_step(s, kc, m_s, acc, h)

    @pl.when(kj < last)
    def _():
        update(False)

    @pl.when(kj == last)
    def _():
        update(True)
        o_ref[...] = jnp.concatenate([_softmax_result(acc[h]) for h in range(D_HEADS)], axis=1).astype(BF16)


def _attn_prompt(qcat, kcat, kcat_t, nb, seq, tq, tk):
    nq, nk = seq // tq, seq // tk
    pairs = [(qi, kj) for qi in range(nq) for kj in range((qi * tq + tq - 1) // tk + 1)]
    qi_tbl = jnp.asarray([p[0] for p in pairs], jnp.int32)
    kj_tbl = jnp.asarray([p[1] for p in pairs], jnp.int32)
    grid_spec = pltpu.PrefetchScalarGridSpec(
        num_scalar_prefetch=2,
        grid=(nb, len(pairs)),
        in_specs=[pl.BlockSpec((tq, D_HEADS * 2 * LANES), lambda b, s, qt, kt: (b * nq + qt[s], 0)),
                  pl.BlockSpec((tk, 2 * LANES), lambda b, s, qt, kt: (b * nk + kt[s], 0)),
                  pl.BlockSpec((2 * LANES, tk), lambda b, s, qt, kt: (0, b * nk + kt[s]))],
        out_specs=pl.BlockSpec((tq, D_HEADS * KV_LORA), lambda b, s, qt, kt: (b * nq + qt[s], 0)),
        scratch_shapes=[pltpu.VMEM((D_HEADS, tq, LANES), F32), pltpu.VMEM((D_HEADS, tq, 2 * LANES), F32)])
    return pl.pallas_call(
        functools.partial(_attn_prompt_kernel, tq=tq, tk=tk),
        grid_spec=grid_spec,
        out_shape=jax.ShapeDtypeStruct((nb * seq, D_HEADS * KV_LORA), BF16),
        compiler_params=_cparams(("arbitrary", "arbitrary")),
        name="attn_prompt",
    )(qi_tbl, kj_tbl, qcat, kcat, kcat_t)


def _attn_paged_kernel(pt_ref, q_ref, knew_ref, ckv_hbm, kpe_hbm, o_ref, ckv_buf, kpe_buf, sems, q_s, m_s, l_s,
                       acc, *, ls, gp):
    b, j = pl.program_id(0), pl.program_id(1)
    nbatch, ngroups = pl.num_programs(0), pl.num_programs(1)
    step = b * ngroups + j
    slot = step % 2

    def page_copies(bb, jj, sl):
        out = []
        for g in range(gp):
            page = pt_ref[bb, jj * gp + g]
            toks = pl.ds(g * PAGE_SIZE, PAGE_SIZE)
            out.append(pltpu.make_async_copy(ckv_hbm.at[page], ckv_buf.at[sl, toks], sems.at[sl]))
            out.append(pltpu.make_async_copy(kpe_hbm.at[page], kpe_buf.at[sl, :, toks], sems.at[sl]))
        return out

    @pl.when(step == 0)
    def _():
        for c in page_copies(b, j, slot):
            c.start()

    @pl.when(step + 1 < nbatch * ngroups)
    def _():
        nxt = step + 1
        for c in page_copies(nxt // ngroups, nxt % ngroups, 1 - slot):
            c.start()

    rows = D_HEADS * ls

    @pl.when(j == 0)
    def _():
        qc = q_ref[0]
        q_s[...] = jnp.concatenate(
            [qc[:, h * 2 * LANES:(h + 1) * 2 * LANES] for h in range(D_HEADS)], axis=0).astype(BF16)
        m_s[...] = jnp.full(m_s.shape, NEG_INF, F32)
        l_s[...] = jnp.zeros(l_s.shape, F32)
        acc[...] = jnp.zeros(acc.shape, F32)

    for c in page_copies(b, j, slot):
        c.wait()

    nt = (((1,), (1,)), ((), ()))
    q_lat = q_s[:, :KV_LORA]
    q_pe = q_s[:, KV_LORA:KV_LORA + D_ROPE]

    def online(i, s, vals):
        m_old = m_s[i]
        m_new = jnp.maximum(m_old, jnp.max(s, axis=-1, keepdims=True))
        p = jnp.exp2((s - jnp.tile(m_new, (1, s.shape[1] // LANES))) * EXP2_SCALE)
        corr = jnp.exp2((m_old - m_new) * EXP2_SCALE)
        l_s[i] = l_s[i] * corr + jnp.sum(p, axis=-1, keepdims=True)
        acc[i] = acc[i] * corr + jnp.dot(p.astype(BF16), vals, preferred_element_type=F32)
        m_s[i] = m_new

    part = gp * PAGE_SIZE // DECODE_SPLIT
    for i in range(DECODE_SPLIT):
        ck = ckv_buf[slot, i * part:(i + 1) * part, :].astype(BF16)
        kp_t = kpe_buf[slot, :, i * part:(i + 1) * part].astype(BF16)
        online(i, lax.dot_general(q_lat, ck, nt, preferred_element_type=F32)
               + jnp.dot(q_pe, kp_t, preferred_element_type=F32), ck)

    @pl.when(j == ngroups - 1)
    def _():
        kn = knew_ref[0]
        pad = jnp.concatenate([kn, jnp.zeros((LANES - ls, 2 * LANES), F32)], axis=0).astype(BF16)
        sn = lax.dot_general(q_s[...], pad, nt, preferred_element_type=F32)
        tq_pos = lax.broadcasted_iota(jnp.int32, (rows, LANES), 0) % ls
        tk_pos = lax.broadcasted_iota(jnp.int32, (rows, LANES), 1)
        online(0, jnp.where(tq_pos >= tk_pos, sn, NEG_INF), pad[:, :KV_LORA])
        m_all = m_s[0]
        for i in range(1, DECODE_SPLIT):
            m_all = jnp.maximum(m_all, m_s[i])
        num, den = jnp.zeros((rows, KV_LORA), F32), jnp.zeros((rows, LANES), F32)
        for i in range(DECODE_SPLIT):
            w = jnp.exp2((m_s[i] - m_all) * EXP2_SCALE)
            num, den = num + acc[i] * w, den + l_s[i] * w
        o = num / den
        o_ref[0] = jnp.concatenate([o[h * ls:(h + 1) * ls] for h in range(D_HEADS)], axis=1)


def _attn_paged(page_table, qcat3, knew3, cache_ckv, cache_kpe_t, gp):
    nbatch, ls = qcat3.shape[0], qcat3.shape[1]
    n_pages = page_table.shape[1]
    rows = D_HEADS * ls
    grid_spec = pltpu.PrefetchScalarGridSpec(
        num_scalar_prefetch=1,
        grid=(nbatch, n_pages // gp),
        in_specs=[pl.BlockSpec((1, ls, D_HEADS * 2 * LANES), lambda b, j, pt: (b, 0, 0)),
                  pl.BlockSpec((1, ls, 2 * LANES), lambda b, j, pt: (b, 0, 0)),
                  pl.BlockSpec(memory_space=pl.ANY), pl.BlockSpec(memory_space=pl.ANY)],
        out_specs=pl.BlockSpec((1, ls, D_HEADS * KV_LORA), lambda b, j, pt: (b, 0, 0)),
        scratch_shapes=[pltpu.VMEM((2, gp * PAGE_SIZE, KV_LORA), F32), pltpu.VMEM((2, D_ROPE, gp * PAGE_SIZE), F32),
                        pltpu.SemaphoreType.DMA((2,)), pltpu.VMEM((rows, 2 * LANES), BF16),
                        pltpu.VMEM((DECODE_SPLIT, rows, LANES), F32), pltpu.VMEM((DECODE_SPLIT, rows, LANES), F32),
                        pltpu.VMEM((DECODE_SPLIT, rows, KV_LORA), F32)])
    return pl.pallas_call(
        functools.partial(_attn_paged_kernel, ls=ls, gp=gp),
        grid_spec=grid_spec,
        out_shape=jax.ShapeDtypeStruct((nbatch, ls, D_HEADS * KV_LORA), F32),
        compiler_params=_cparams(("arbitrary", "arbitrary")),
        name="attn_paged",
    )(page_table, qcat3, knew3, cache_ckv, cache_kpe_t)


def _odd_out_kernel(x_ref, yc_ref, ol_ref, wuv_ref, wo_ref, lg_ref, lb_ref, o_ref):
    yd = _dot(ol_ref[...], wuv_ref[...])
    y = _dot(jnp.concatenate([yc_ref[...], yd], axis=1), wo_ref[...])
    o_ref[...] = _layernorm(ALPHA * x_ref[...] + y, lg_ref[...], lb_ref[...])


def _odd_out(x2d, yc, olat, tm, arrs):
    rows = x2d.shape[0]
    row_spec = lambda w: pl.BlockSpec((tm, w), lambda i: (i, 0))
    return pl.pallas_call(
        _odd_out_kernel,
        grid=(rows // tm,),
        in_specs=[row_spec(D_MODEL), row_spec(C_WIDTH), row_spec(D_HEADS * KV_LORA)]
                 + [_const_spec(a.shape) for a in arrs],
        out_specs=row_spec(D_MODEL),
        out_shape=jax.ShapeDtypeStruct((rows, D_MODEL), F32),
        compiler_params=_cparams(("arbitrary",)),
        name="odd_out",
    )(x2d, yc, olat, *arrs)


def _block_diag(blocks):
    n, r, c = blocks.shape
    eye = jnp.eye(n, dtype=blocks.dtype)
    return (eye[:, None, :, None] * blocks[:, :, None, :]).reshape(n * r, n * c)


def _rope_tables(positions):
    half = D_ROPE // 2
    inv = ROPE_BASE ** (-jnp.arange(half, dtype=F32) / half)
    ang = positions[:, None] * inv
    cos, sin = jnp.cos(ang), jnp.sin(ang)
    reps = LANES // D_ROPE
    return jnp.tile(jnp.concatenate([cos, cos], axis=1), (1, reps)), jnp.tile(jnp.concatenate([-sin, sin], axis=1), (1, reps))


def _tile_rows(n, cap):
    t = min(n, cap)
    assert n % t == 0, (n, cap)
    return t


def kernel(x_prompt, x_sample, state_pool_buf, state_rwkv, state_shift, cache_ckv, cache_kpe, page_table, ln_g, ln_b, w_in_even, pool_w, pool_scale, sgu_ln_g, sgu_ln_b, sgu_w, sgu_b, w_out_even, w_in_odd, rwkv_mu, rwkv_w0, rwkv_w2, rwkv_a0, rwkv_a2, rwkv_g2, rwkv_kk, rwkv_ka, rwkv_rk, rwkv_gn_g, rwkv_gn_b, mla_gq, mla_gkv, mla_wuq, mla_wuk, mla_wuv, w_out_odd, w_router, router_bias, w_gate, w_up, w_down):
    bp, lp, _ = x_prompt.shape
    bs, ls, _ = x_sample.shape
    n_pages = page_table.shape[1]
    past_len = n_pages * PAGE_SIZE
    assert ls == 8 and lp % SGU_CHUNK == 0 and lp % RWKV_CHUNK == 0
    rows_p, rows_s = bp * lp, bs * ls
    tm_p = _tile_rows(lp, 512)
    tm_s = _tile_rows(rows_s, 512)
    tm_moe_p = _tile_rows(rows_p, 512)
    tm_moe_s = _tile_rows(rows_s, 512)

    xp = x_prompt.reshape(rows_p, D_MODEL)
    xs = x_sample.reshape(rows_s, D_MODEL)

    wr_pad = jnp.pad(w_router, ((0, 0), (0, LANES - N_EXPERTS)))
    rb_pad = jnp.pad(router_bias, (0, LANES - N_EXPERTS)).reshape(1, LANES)
    wg, wu, wd = w_gate.astype(BF16), w_up.astype(BF16), w_down.astype(BF16)

    outs = {}
    for layer in range(DEPTH):
        i = layer // 2
        g1, b1 = ln_g[layer, 0], ln_b[layer, 0]
        if layer % 2 == 0:
            wi_split, wo_split = _split_weight(w_in_even[i]), _split_weight(w_out_even[i])
            wts_p = _even_weights(wi_split, pool_w[i], pool_scale[i], sgu_ln_g[i], sgu_ln_b[i], sgu_w[i],
                                  sgu_b[i].T, wo_split, g1, b1)
            reps = SGU_CHUNK // ls
            w_small = jnp.stack([_block_diag(jnp.broadcast_to(sgu_w[i][h, :ls, :ls], (reps, ls, ls)))
                                 for h in range(B_HEADS)])
            b_small = jnp.tile(sgu_b[i][:, :ls], (1, reps)).T
            wts_s = _even_weights(wi_split, pool_w[i], pool_scale[i], sgu_ln_g[i], sgu_ln_b[i], w_small,
                                  b_small, wo_split, g1, b1)
            xp, tail = _even_prompt(xp, bp, lp, _tile_rows(lp, 256), wts_p)
            bufpad = jnp.pad(state_pool_buf[i], ((0, 0), (HALO - POOL_BUF, 0), (0, 0)))
            xs, a_new, vn_s = _even_sample(xs, bufpad, ls, tm_s, wts_s)
            outs.setdefault("pool_p", []).append(tail.reshape(bp, HALO, A_WIDTH)[:, HALO - POOL_BUF:])
            ext = jnp.concatenate([state_pool_buf[i], a_new.reshape(bs, ls, A_WIDTH)], axis=1)
            outs.setdefault("pool_s", []).append(ext[:, -POOL_BUF:])
            outs.setdefault("sgu_s", []).append(vn_s.reshape(bs, ls, B_WIDTH))
        else:
            nope = D_HEADS * D_NOPE
            wq_cols = mla_wuq[i].reshape(Q_LORA, D_HEADS, D_NOPE + D_ROPE)
            wuq_perm = jnp.concatenate([wq_cols[:, :, :D_NOPE].reshape(Q_LORA, nope),
                                        wq_cols[:, :, D_NOPE:].reshape(Q_LORA, D_HEADS * D_ROPE)], axis=1)
            wuk_bd = _block_diag(jnp.swapaxes(mla_wuk[i], 1, 2))
            wuv_bd = _block_diag(mla_wuv[i])
            w_in = w_in_odd[i]
            w_in_pad = jnp.concatenate([w_in[:, :C_COLS + Q_LORA + KV_LORA],
                                        jnp.tile(w_in[:, C_COLS + Q_LORA + KV_LORA:], (1, LANES // D_ROPE))], axis=1)
            in_arrs = [w_in_pad.astype(BF16), mla_gq[i].reshape(1, -1), mla_gkv[i].reshape(1, -1),
                       wuq_perm.astype(BF16), wuk_bd.astype(BF16)]
            cos_p, sin_p = _rope_tables(jnp.arange(lp, dtype=F32))
            cos_s, sin_s = _rope_tables(past_len + jnp.arange(ls, dtype=F32))
            cos_s, sin_s = jnp.tile(cos_s, (tm_s // ls, 1)), jnp.tile(sin_s, (tm_s // ls, 1))
            pc_p, qcat_p, ckv_p, kpe_p, kcat_p, kcat_t_p = _odd_in(xp, cos_p, sin_p, tm_p, lp // tm_p, in_arrs, BF16)
            pc_s, qcat_s, ckv_s, kpe_s, kcat_s, _ = _odd_in(xs, cos_s, sin_s, tm_s, 1, in_arrs, F32)

            z64 = jnp.zeros((C_DECAY_LORA, C_WIDTH), F32)
            wa2 = jnp.concatenate([jnp.concatenate([rwkv_w2[i], z64], axis=1),
                                   jnp.concatenate([z64, rwkv_a2[i]], axis=1)], axis=0)
            row = lambda v: v.reshape(1, -1)
            rw_arrs = [row(rwkv_mu[i]), row(rwkv_w0[i]), wa2.astype(BF16), row(rwkv_a0[i]), rwkv_g2[i].astype(BF16),
                       row(rwkv_kk[i]), row(rwkv_ka[i]), row(rwkv_rk[i]), row(rwkv_gn_g[i]), row(rwkv_gn_b[i]),
                       _block_diag(jnp.ones((C_HEADS, C_HEAD_DIM, C_HEAD_DIM), BF16))]
            zero_state = jnp.zeros((bp, C_HEADS, C_HEAD_DIM, C_HEAD_DIM), F32)
            yc_p, st_p = _rwkv(pc_p, jnp.zeros((bp, 1, C_COLS), F32), _state_to_groups(zero_state),
                               bp, lp, RWKV_CHUNK, RWKV_BLOCK, rw_arrs)
            yc_s, st_s = _rwkv(pc_s, state_shift[i], _state_to_groups(state_rwkv[i]), bs, ls, RWKV_CHUNK_SHORT, 1,
                               rw_arrs)

            ol_p = _attn_prompt(qcat_p, kcat_p, kcat_t_p, bp, lp, _tile_rows(lp, 256), _tile_rows(lp, 512))
            ol_s = _attn_paged(page_table, qcat_s.reshape(bs, ls, -1), kcat_s.reshape(bs, ls, -1),
                               cache_ckv[i], jnp.swapaxes(cache_kpe[i], 1, 2), math.gcd(n_pages, PAGES_PER_STEP))

            out_arrs = [wuv_bd.astype(BF16), w_out_odd[i].astype(BF16), g1.reshape(1, -1), b1.reshape(1, -1)]
            xp_new = _odd_out(xp, yc_p, ol_p, tm_p, out_arrs)
            xs_new = _odd_out(xs, yc_s, ol_s.reshape(rows_s, -1), tm_s, out_arrs)

            outs.setdefault("rwkv_p", []).append(_groups_to_state(st_p))
            outs.setdefault("rwkv_s", []).append(_groups_to_state(st_s))
            outs.setdefault("shift_p", []).append(pc_p.reshape(bp, lp, C_COLS)[:, -1:])
            outs.setdefault("shift_s", []).append(pc_s.reshape(bs, ls, C_COLS)[:, -1:])
            outs.setdefault("ckv_p", []).append(ckv_p.reshape(bp, lp, KV_LORA))
            outs.setdefault("ckv_s", []).append(ckv_s.reshape(bs, ls, KV_LORA))
            outs.setdefault("kpe_p", []).append(kpe_p.reshape(bp, lp, D_ROPE))
            outs.setdefault("kpe_s", []).append(kpe_s.reshape(bs, ls, D_ROPE))
            xp, xs = xp_new, xs_new
        g2, b2 = ln_g[layer, 1], ln_b[layer, 1]
        xp = _moe(xp, tm_moe_p, wr_pad, rb_pad, wg[layer], wu[layer], wd[layer], g2, b2)
        xs = _moe(xs, tm_moe_s, wr_pad, rb_pad, wg[layer], wu[layer], wd[layer], g2, b2)

    st = lambda k: jnp.stack(outs[k])
    return (xp.reshape(bp, lp, D_MODEL), xs.reshape(bs, ls, D_MODEL), st("pool_p"), st("pool_s"), st("sgu_s"),
            st("rwkv_p"), st("rwkv_s"), st("shift_p"), st("shift_s"), st("ckv_p"), st("ckv_s"), st("kpe_p"),
            st("kpe_s"))
```
